```python
import jax, jax.numpy as jnp
from jax import lax
import numpy as np

D_MODEL = 1024
BATCH = 16
SEQ = 2048
DEPTH = 1

RET_HEADS = 4
RET_QK_DIM = D_MODEL // RET_HEADS
RET_V_DIM = 2 * RET_QK_DIM
RET_QK = RET_HEADS * RET_QK_DIM
RET_V = RET_HEADS * RET_V_DIM
CHUNK = 128
ROPE_BASE = 10000.0
CONV_CH = D_MODEL
CONV_WIDTH = 31
D_FF = 2816
EPS = 1e-6
N_SUB = 3
MAX_POS_OFFSET = 4096
IN_SPLITS = [RET_QK, RET_QK, RET_V, RET_V, CONV_CH, CONV_CH, D_MODEL, D_MODEL]
IN_COLS = sum(IN_SPLITS)

kernel_name = "hybrid_retention_conformer_macaron_adaln"


def rms_norm(x, g):
    xf = x.astype(jnp.float32)
    y = xf * lax.rsqrt(jnp.mean(xf * xf, axis=-1, keepdims=True) + EPS)
    return (y * g.astype(jnp.float32)).astype(x.dtype)


def layer_norm(x, g, b):
    xf = x.astype(jnp.float32)
    mu = jnp.mean(xf, axis=-1, keepdims=True)
    var = jnp.mean(jnp.square(xf - mu), axis=-1, keepdims=True)
    y = (xf - mu) * lax.rsqrt(var + EPS)
    return (y * g.astype(jnp.float32) + b.astype(jnp.float32)).astype(x.dtype)


def modulate(xn, shift, scale):
    return xn * (1.0 + scale[:, None, :]) + shift[:, None, :]


def swiglu_ffn(x, w_gu, w_d):
    gate, up = jnp.split(x @ w_gu, 2, axis=-1)
    return (jax.nn.silu(gate) * up) @ w_d


def rope(x, positions):
    d = x.shape[-1]
    half = d // 2
    freqs = jnp.power(ROPE_BASE, -jnp.arange(0, half, dtype=jnp.float32) * (2.0 / d))
    ang = positions.astype(jnp.float32)[..., None] * freqs
    cos = jnp.cos(ang)[:, :, None, :]
    sin = jnp.sin(ang)[:, :, None, :]
    x1, x2 = x[..., :half], x[..., half:]
    return jnp.concatenate([x1 * cos - x2 * sin, x1 * sin + x2 * cos], axis=-1)


def retention_chunkwise(q, k, v):
    B, S, H, dk = q.shape
    dv = v.shape[-1]
    n_chunks = S // CHUNK
    log_gamma = jnp.log1p(-jnp.power(2.0, -5.0 - jnp.arange(H, dtype=jnp.float32)))
    idx = jnp.arange(CHUNK, dtype=jnp.float32)
    diff = idx[:, None] - idx[None, :]
    intra = jnp.where(diff[None] >= 0,
                      jnp.exp(log_gamma[:, None, None] * jnp.maximum(diff, 0.0)[None]), 0.0)
    xi = jnp.exp(log_gamma[:, None] * (idx + 1.0))[None, :, :, None]
    zeta = jnp.exp(log_gamma[:, None] * (CHUNK - 1.0 - idx))[None, :, :, None]
    gamma_c = jnp.exp(log_gamma * CHUNK)[None, :, None, None]

    def to_chunks(t):
        d = t.shape[-1]
        return t.reshape(B, n_chunks, CHUNK, H, d).transpose(1, 0, 3, 2, 4)

    def step(state, xs):
        qc, kc, vc = xs
        s = jnp.einsum('bhqd,bhkd->bhqk', qc, kc) * intra
        o = (jnp.einsum('bhqk,bhkv->bhqv', s, vc)
             + jnp.einsum('bhqd,bhdv->bhqv', qc, state) * xi)
        state = state * gamma_c + jnp.einsum('bhkd,bhkv->bhdv', kc * zeta, vc)
        return state, o

    state0 = jnp.zeros((B, H, dk, dv), jnp.float32)
    _, o = lax.scan(step, state0, (to_chunks(q), to_chunks(k), to_chunks(v)))
    return o.transpose(1, 0, 3, 2, 4).reshape(B, S, H, dv)


def head_group_norm(o, g, b):
    mu = jnp.mean(o, axis=-1, keepdims=True)
    var = jnp.mean(jnp.square(o - mu), axis=-1, keepdims=True)
    y = ((o - mu) * lax.rsqrt(var + EPS)).reshape(o.shape[0], o.shape[1], -1)
    return y * g.astype(jnp.float32) + b.astype(jnp.float32)


def hybrid_mixer(xn, positions, w_in, ret_gn_g, ret_gn_b, w_ret_o, w_dw, b_dw,
                 conv_ln_g, conv_ln_b, w_conv_o, b_conv_o, w_out):
    B, S, _ = xn.shape
    z = xn @ w_in
    cuts = list(np.cumsum(IN_SPLITS)[:-1])
    q, k, v, g_ret, glu_a, glu_b, gate_ret, gate_conv = jnp.split(z, cuts, axis=-1)

    qf = rope(q.astype(jnp.float32).reshape(B, S, RET_HEADS, RET_QK_DIM), positions)
    kf = rope(k.astype(jnp.float32).reshape(B, S, RET_HEADS, RET_QK_DIM), positions) * (RET_QK_DIM ** -0.5)
    vf = v.astype(jnp.float32).reshape(B, S, RET_HEADS, RET_V_DIM)
    o = head_group_norm(retention_chunkwise(qf, kf, vf), ret_gn_g, ret_gn_b).astype(xn.dtype)
    y_ret = (jax.nn.silu(g_ret) * o) @ w_ret_o

    u = glu_a * jax.nn.sigmoid(glu_b)
    u = lax.conv_general_dilated(u, w_dw.astype(u.dtype), window_strides=(1,),
                                 padding=[(CONV_WIDTH - 1, 0)],
                                 dimension_numbers=('NWC', 'WIO', 'NWC'),
                                 feature_group_count=CONV_CH) + b_dw
    u = jax.nn.silu(layer_norm(u, conv_ln_g, conv_ln_b))
    y_conv = u @ w_conv_o + b_conv_o

    merged = jax.nn.sigmoid(gate_ret) * y_ret + jax.nn.sigmoid(gate_conv) * y_conv
    return merged @ w_out


def setup_inputs(seed: int = 0) -> dict:
    key = jax.random.key(seed)
    ks = iter(jax.random.split(key, 32))
    f32 = jnp.float32

    def w(shape, fan_in, scale=1.0):
        return jax.random.normal(next(ks), shape, f32) * (scale * fan_in ** -0.5)

    def gain(shape):
        return 1.0 + 0.05 * jax.random.normal(next(ks), shape, f32)

    def bias(shape):
        return 0.02 * jax.random.normal(next(ks), shape, f32)

    L = DEPTH
    x = jax.random.normal(next(ks), (BATCH, SEQ, D_MODEL), f32)
    c = jax.random.normal(next(ks), (BATCH, D_MODEL), f32)
    offset = jax.random.randint(next(ks), (BATCH, 1), 0, MAX_POS_OFFSET, dtype=jnp.int32)
    positions = offset + jnp.arange(SEQ, dtype=jnp.int32)[None, :]
    return {
        "x": x,
        "c": c,
        "positions": positions,
        "w_mod": w((L, D_MODEL, 3 * N_SUB * D_MODEL), D_MODEL),
        "b_mod": bias((L, 3 * N_SUB * D_MODEL)),
        "g_norm1": gain((L, D_MODEL)),
        "w_ffn1_gu": w((L, D_MODEL, 2 * D_FF), D_MODEL),
        "w_ffn1_d": w((L, D_FF, D_MODEL), D_FF),
        "g_norm2": gain((L, D_MODEL)),
        "w_in": w((L, D_MODEL, IN_COLS), D_MODEL),
        "ret_gn_g": gain((L, RET_V)),
        "ret_gn_b": bias((L, RET_V)),
        "w_ret_o": w((L, RET_V, D_MODEL), RET_V),
        "w_dw": w((L, CONV_WIDTH, 1, CONV_CH), CONV_WIDTH),
        "b_dw": bias((L, CONV_CH)),
        "conv_ln_g": gain((L, CONV_CH)),
        "conv_ln_b": bias((L, CONV_CH)),
        "w_conv_o": w((L, CONV_CH, D_MODEL), CONV_CH),
        "b_conv_o": bias((L, D_MODEL)),
        "w_out": w((L, D_MODEL, D_MODEL), D_MODEL),
        "g_norm3": gain((L, D_MODEL)),
        "w_ffn2_gu": w((L, D_MODEL, 2 * D_FF), D_MODEL),
        "w_ffn2_d": w((L, D_FF, D_MODEL), D_FF),
        "g_normf": gain((D_MODEL,)),
    }


def reference(x, c, positions, w_mod, b_mod, g_norm1, w_ffn1_gu, w_ffn1_d, g_norm2,
              w_in, ret_gn_g, ret_gn_b, w_ret_o, w_dw, b_dw, conv_ln_g, conv_ln_b,
              w_conv_o, b_conv_o, w_out, g_norm3, w_ffn2_gu, w_ffn2_d, g_normf):
    h = x
    c_act = jax.nn.silu(c)
    for l in range(DEPTH):
        mod = c_act @ w_mod[l] + b_mod[l]
        (sh1, sc1, gt1, sh2, sc2, gt2, sh3, sc3, gt3) = jnp.split(mod, 3 * N_SUB, axis=-1)
        xn = modulate(rms_norm(h, g_norm1[l]), sh1, sc1)
        h = h + 0.5 * gt1[:, None, :] * swiglu_ffn(xn, w_ffn1_gu[l], w_ffn1_d[l])
        xn = modulate(rms_norm(h, g_norm2[l]), sh2, sc2)
        h = h + gt2[:, None, :] * hybrid_mixer(xn, positions, w_in[l], ret_gn_g[l], ret_gn_b[l],
                                               w_ret_o[l], w_dw[l], b_dw[l], conv_ln_g[l],
                                               conv_ln_b[l], w_conv_o[l], b_conv_o[l], w_out[l])
        xn = modulate(rms_norm(h, g_norm3[l]), sh3, sc3)
        h = h + 0.5 * gt3[:, None, :] * swiglu_ffn(xn, w_ffn2_gu[l], w_ffn2_d[l])
    return rms_norm(h, g_normf)
```

```python
import functools

import numpy as np
import jax
import jax.numpy as jnp
from jax import lax
from jax.experimental import pallas as pl
from jax.experimental.pallas import tpu as pltpu

F32 = jnp.float32
BF16 = jnp.bfloat16

EPS = 1e-6
N_SUB = 3
RET_HEADS = 4
CONV_WIDTH = 31
ROPE_BASE = 10000.0
RET_CHUNK = 256
CONV_HALO = 32
LANES = 128
VMEM_LIMIT_BYTES = 56 * 1024 * 1024


def _sigmoid(x):
    return 0.5 * jnp.tanh(0.5 * x) + 0.5


def _silu(x):
    return x * _sigmoid(x)


def _rms_norm(x, g):
    return x * lax.rsqrt(jnp.mean(x * x, axis=-1, keepdims=True) + EPS) * g


def _dot(a, b):
    return jnp.dot(a, b, preferred_element_type=F32)


def _resident(shape):
    zeros = (0,) * len(shape)
    return pl.BlockSpec(shape, lambda *_: zeros, pipeline_mode=pl.Buffered(1))


def _params(*semantics):
    return pltpu.CompilerParams(dimension_semantics=semantics, vmem_limit_bytes=VMEM_LIMIT_BYTES)


def _mod_kernel(c_ref, w_ref, b_ref, o_ref):
    c = c_ref[...]
    o_ref[...] = _dot(_silu(c).astype(BF16), w_ref[...].astype(BF16)) + b_ref[...]


def _mod(c, w_mod, b_mod, tn=1024):
    B, D = c.shape
    N = w_mod.shape[1]
    return pl.pallas_call(
        _mod_kernel,
        grid=(N // tn,),
        in_specs=[_resident((B, D)),
                  pl.BlockSpec((D, tn), lambda j: (0, j)),
                  pl.BlockSpec((1, tn), lambda j: (0, j))],
        out_specs=pl.BlockSpec((B, tn), lambda j: (0, j)),
        out_shape=jax.ShapeDtypeStruct((B, N), F32),
        compiler_params=_params("parallel"),
        name="mod",
    )(c, w_mod, b_mod.reshape(1, N))


def _mod_slices(mod_ref, sub, D):
    base = 3 * sub * D
    return (mod_ref[:, base:base + D], mod_ref[:, base + D:base + 2 * D],
            mod_ref[:, base + 2 * D:base + 3 * D])


def _ffn_kernel(x_ref, mod_ref, g_ref, wgu_ref, wd_ref, gf_ref, o_ref, *, sub, ck, final_norm):
    D = x_ref.shape[-1]
    Fh = wd_ref.shape[0]
    x = x_ref[...]
    shift, scale, gate = _mod_slices(mod_ref, sub, D)
    xb = (_rms_norm(x, g_ref[...]) * (1.0 + scale) + shift).astype(BF16)
    acc = jnp.zeros(x.shape, F32)
    for j in range(Fh // ck):
        g = _dot(xb, wgu_ref[:, j * ck:(j + 1) * ck])
        u = _dot(xb, wgu_ref[:, Fh + j * ck:Fh + (j + 1) * ck])
        acc = acc + _dot((_silu(g) * u).astype(BF16), wd_ref[j * ck:(j + 1) * ck, :])
    h = x + (0.5 * gate) * acc
    if final_norm:
        h = _rms_norm(h, gf_ref[...])
    o_ref[...] = h


def _ffn(x, mod3, g_norm, w_gu, w_d, g_final, *, sub, final_norm, tm=512, ck=256):
    B, S, D = x.shape
    Fh = w_d.shape[0]
    nt = S // tm
    kern = functools.partial(_ffn_kernel, sub=sub, ck=ck, final_norm=final_norm)
    return pl.pallas_call(
        kern,
        grid=(B, nt),
        in_specs=[pl.BlockSpec((None, tm, D), lambda b, i: (b, i, 0)),
                  pl.BlockSpec((None, 1, mod3.shape[-1]), lambda b, i: (b, 0, 0)),
                  _resident((1, D)),
                  _resident((D, 2 * Fh)),
                  _resident((Fh, D)),
                  _resident((1, D))],
        out_specs=pl.BlockSpec((None, tm, D), lambda b, i: (b, i, 0)),
        out_shape=jax.ShapeDtypeStruct((B, S, D), F32),
        compiler_params=_params("parallel", "parallel"),
        name=f"ffn{sub}",
    )(x, mod3, g_norm.reshape(1, D), w_gu, w_d, g_final.reshape(1, D))


def _inproj_kernel(x_ref, mod_ref, pos_ref, freq_ref, g_ref, w_ref,
                   q_ref, k_ref, v_ref, sg_ref, u_ref, gr_ref, gc_ref, *, k_scale):
    D = x_ref.shape[-1]
    dk = D // RET_HEADS
    half = dk // 2
    x = x_ref[...]
    shift, scale, _ = _mod_slices(mod_ref, 1, D)
    xb = (_rms_norm(x, g_ref[...]) * (1.0 + scale) + shift).astype(BF16)

    def proj(c0, n=D):
        return _dot(xb, w_ref[:, c0:c0 + n])

    ang = pos_ref[...].astype(F32) * freq_ref[...]
    cos = jnp.cos(ang)
    sin = jnp.sin(ang)

    def rope_store(z, out_ref, mult):
        for h in range(RET_HEADS):
            x1 = z[:, h * dk:h * dk + half]
            x2 = z[:, h * dk + half:(h + 1) * dk]
            out_ref[:, h * dk:h * dk + half] = ((x1 * cos - x2 * sin) * mult).astype(BF16)
            out_ref[:, h * dk + half:(h + 1) * dk] = ((x1 * sin + x2 * cos) * mult).astype(BF16)

    rope_store(proj(0), q_ref, 1.0)
    rope_store(proj(D), k_ref, k_scale)
    for j in range(2):
        v_ref[:, j * D:(j + 1) * D] = proj(2 * D + j * D).astype(BF16)
    for j in range(2):
        sg_ref[:, j * D:(j + 1) * D] = _silu(proj(4 * D + j * D)).astype(BF16)
    u_ref[...] = (proj(6 * D) * _sigmoid(proj(7 * D))).astype(BF16)
    gr_ref[...] = _sigmoid(proj(8 * D)).astype(BF16)
    gc_ref[...] = _sigmoid(proj(9 * D)).astype(BF16)


def _inproj(h, mod3, pos3, freqs, g_norm, w_in, tm=512):
    B, S, D = h.shape
    dk = D // RET_HEADS
    nt = S // tm
    row = lambda n: pl.BlockSpec((None, tm, n), lambda b, i: (b, i, 0))
    out = lambda n: jax.ShapeDtypeStruct((B, S, n), BF16)
    kern = functools.partial(_inproj_kernel, k_scale=float(dk) ** -0.5)
    return pl.pallas_call(
        kern,
        grid=(B, nt),
        in_specs=[row(D),
                  pl.BlockSpec((None, 1, mod3.shape[-1]), lambda b, i: (b, 0, 0)),
                  row(1),
                  _resident((1, dk // 2)),
                  _resident((1, D)),
                  _resident(w_in.shape)],
        out_specs=[row(D), row(D), row(2 * D), row(2 * D), row(D), row(D), row(D)],
        out_shape=[out(D), out(D), out(2 * D), out(2 * D), out(D), out(D), out(D)],
        compiler_params=_params("parallel", "parallel"),
        name="inproj",
    )(h, mod3, pos3, freqs, g_norm.reshape(1, D), w_in)


def _retention_kernel(q_ref, k_ref, v_ref, sg_ref, intra_ref, xi_ref, zeta_ref, gng_ref, gnb_ref,
                      o_ref, state_ref, *, gamma_c):
    dk = q_ref.shape[-1] // RET_HEADS
    dv = v_ref.shape[-1] // RET_HEADS

    @pl.when(pl.program_id(1) == 0)
    def _():
        state_ref[...] = jnp.zeros(state_ref.shape, F32)

    for h in range(RET_HEADS):
        qh = q_ref[:, h * dk:(h + 1) * dk]
        kh = k_ref[:, h * dk:(h + 1) * dk]
        vh = v_ref[:, h * dv:(h + 1) * dv]
        s = lax.dot_general(qh, kh, (((1,), (1,)), ((), ())), preferred_element_type=F32) * intra_ref[h]
        st = state_ref[h]
        o = _dot(s.astype(BF16), vh) + _dot(qh, st.astype(BF16)) * xi_ref[h]
        kz = (kh.astype(F32) * zeta_ref[h]).astype(BF16)
        state_ref[h] = st * gamma_c[h] + lax.dot_general(
            kz, vh, (((0,), (0,)), ((), ())), preferred_element_type=F32)
        mu = jnp.mean(o, axis=-1, keepdims=True)
        d = o - mu
        var = jnp.mean(d * d, axis=-1, keepdims=True)
        y = d * lax.rsqrt(var + EPS) * gng_ref[:, h * dv:(h + 1) * dv] + gnb_ref[:, h * dv:(h + 1) * dv]
        o_ref[:, h * dv:(h + 1) * dv] = (sg_ref[:, h * dv:(h + 1) * dv].astype(F32) * y).astype(BF16)


def _retention_tables(chunk):
    log_gamma = np.log1p(-np.power(2.0, -5.0 - np.arange(RET_HEADS, dtype=np.float64)))
    idx = np.arange(chunk, dtype=np.float64)
    diff = idx[:, None] - idx[None, :]
    intra = np.where(diff[None] >= 0, np.exp(log_gamma[:, None, None] * np.maximum(diff, 0.0)[None]), 0.0)
    xi = np.exp(log_gamma[:, None] * (idx + 1.0))[:, :, None]
    zeta = np.exp(log_gamma[:, None] * (chunk - 1.0 - idx))[:, :, None]
    gamma_c = tuple(float(g) for g in np.exp(log_gamma * chunk))
    return (jnp.asarray(intra, F32), jnp.asarray(xi, F32), jnp.asarray(zeta, F32), gamma_c)


def _retention(q, k, v, sg, gn_g, gn_b, chunk=RET_CHUNK):
    B, S, QK = q.shape
    V = v.shape[-1]
    dk, dv = QK // RET_HEADS, V // RET_HEADS
    intra, xi, zeta, gamma_c = _retention_tables(chunk)
    row = lambda n: pl.BlockSpec((None, chunk, n), lambda b, i: (b, i, 0))
    kern = functools.partial(_retention_kernel, gamma_c=gamma_c)
    return pl.pallas_call(
        kern,
        grid=(B, S // chunk),
        in_specs=[row(QK), row(QK), row(V), row(V),
                  _resident(intra.shape), _resident(xi.shape), _resident(zeta.shape),
                  _resident((1, V)), _resident((1, V))],
        out_specs=row(V),
        out_shape=jax.ShapeDtypeStruct((B, S, V), BF16),
        scratch_shapes=[pltpu.VMEM((RET_HEADS, dk, dv), F32)],
        compiler_params=_params("parallel", "arbitrary"),
        name="retention",
    )(q, k, v, sg, intra, xi, zeta, gn_g.reshape(1, V), gn_b.reshape(1, V))


def _merge_kernel(h_ref, mod_ref, og_ref, ucur_ref, uprev_ref, gr_ref, gc_ref,
                  wdw_ref, bdw_ref, lng_ref, lnb_ref, wro_ref, wco_ref, bco_ref, wout_ref,
                  o_ref, win_ref, conv_ref, *, rb):
    tm, D = h_ref.shape
    _, _, gate = _mod_slices(mod_ref, 1, D)

    prev = uprev_ref[...].astype(F32)
    win_ref[0:CONV_HALO, :] = jnp.where(pl.program_id(1) > 0, prev, 0.0)
    win_ref[CONV_HALO:CONV_HALO + tm, :] = ucur_ref[...].astype(F32)

    off0 = CONV_HALO - (CONV_WIDTH - 1)
    for r0 in range(0, tm, rb):
        for c0 in range(0, D, LANES):
            acc = jnp.broadcast_to(bdw_ref[:, c0:c0 + LANES], (rb, LANES))
            for kk in range(CONV_WIDTH):
                acc = acc + win_ref[r0 + off0 + kk:r0 + off0 + kk + rb, c0:c0 + LANES] * wdw_ref[kk:kk + 1, c0:c0 + LANES]
            conv_ref[r0:r0 + rb, c0:c0 + LANES] = acc

    cv = conv_ref[...]
    mu = jnp.mean(cv, axis=-1, keepdims=True)
    d = cv - mu
    var = jnp.mean(d * d, axis=-1, keepdims=True)
    ln = d * lax.rsqrt(var + EPS) * lng_ref[...] + lnb_ref[...]
    y_conv = _dot(_silu(ln).astype(BF16), wco_ref[...]) + bco_ref[...]
    y_ret = _dot(og_ref[...], wro_ref[...])
    merged = gr_ref[...].astype(F32) * y_ret + gc_ref[...].astype(F32) * y_conv
    o_ref[...] = h_ref[...] + gate * _dot(merged.astype(BF16), wout_ref[...])


def _merge(h, mod3, og, u, gr, gc, w_dw, b_dw, ln_g, ln_b, w_ret_o, w_conv_o, b_conv_o, w_out, tm=256, rb=64):
    B, S, D = h.shape
    V = og.shape[-1]
    nt = S // tm
    hb = tm // CONV_HALO
    row = lambda n: pl.BlockSpec((None, tm, n), lambda b, i: (b, i, 0))
    kern = functools.partial(_merge_kernel, rb=rb)
    return pl.pallas_call(
        kern,
        grid=(B, nt),
        in_specs=[row(D),
                  pl.BlockSpec((None, 1, mod3.shape[-1]), lambda b, i: (b, 0, 0)),
                  row(V), row(D),
                  pl.BlockSpec((None, CONV_HALO, D), lambda b, i: (b, jnp.maximum(i * hb - 1, 0), 0)),
                  row(D), row(D),
                  _resident((CONV_WIDTH, D)), _resident((1, D)), _resident((1, D)), _resident((1, D)),
                  _resident((V, D)), _resident((D, D)), _resident((1, D)), _resident((D, D))],
        out_specs=row(D),
        out_shape=jax.ShapeDtypeStruct((B, S, D), F32),
        scratch_shapes=[pltpu.VMEM((CONV_HALO + tm, D), F32), pltpu.VMEM((tm, D), F32)],
        compiler_params=_params("parallel", "parallel"),
        name="merge",
    )(h, mod3, og, u, u, gr, gc, w_dw, b_dw.reshape(1, D), ln_g.reshape(1, D), ln_b.reshape(1, D),
      w_ret_o, w_conv_o, b_conv_o.reshape(1, D), w_out)


def kernel(x, c, positions, w_mod, b_mod, g_norm1, w_ffn1_gu, w_ffn1_d, g_norm2, w_in, ret_gn_g, ret_gn_b,
           w_ret_o, w_dw, b_dw, conv_ln_g, conv_ln_b, w_conv_o, b_conv_o, w_out, g_norm3, w_ffn2_gu,
           w_ffn2_d, g_normf):
    B, S, D = x.shape
    depth = w_mod.shape[0]
    dk = D // RET_HEADS
    freqs = jnp.asarray(
        np.power(ROPE_BASE, -np.arange(0, dk // 2, dtype=np.float64) * (2.0 / dk)), F32).reshape(1, dk // 2)
    pos3 = positions.reshape(B, S, 1)
    bf = lambda w: w.astype(BF16)

    h = x
    for l in range(depth):
        last = l == depth - 1
        mod3 = _mod(c, w_mod[l], b_mod[l]).reshape(B, 1, 3 * N_SUB * D)
        h = _ffn(h, mod3, g_norm1[l], bf(w_ffn1_gu[l]), bf(w_ffn1_d[l]), g_normf, sub=0, final_norm=False)
        q, k, v, sg, u, gr, gc = _inproj(h, mod3, pos3, freqs, g_norm2[l], bf(w_in[l]))
        og = _retention(q, k, v, sg, ret_gn_g[l], ret_gn_b[l])
        h = _merge(h, mod3, og, u, gr, gc, w_dw[l].reshape(CONV_WIDTH, D), b_dw[l], conv_ln_g[l],
                   conv_ln_b[l], bf(w_ret_o[l]), bf(w_conv_o[l]), b_conv_o[l], bf(w_out[l]))
        h = _ffn(h, mod3, g_norm3[l], bf(w_ffn2_gu[l]), bf(w_ffn2_d[l]), g_normf, sub=2, final_norm=last)
    if depth == 0:
        h = x
    return h
```

```python
import functools

import numpy as np
import jax
import jax.numpy as jnp
from jax import lax
from jax.experimental import pallas as pl
from jax.experimental.pallas import tpu as pltpu

F32 = jnp.float32
BF16 = jnp.bfloat16

EPS = 1e-6
N_SUB = 3
RET_HEADS = 4
CONV_WIDTH = 31
ROPE_BASE = 10000.0
RET_CHUNK = 256
CONV_HALO = 32
LANES = 128
SUBLANES = 8
VMEM_LIMIT_BYTES = 56 * 1024 * 1024


def _sigmoid(x):
    return 0.5 * jnp.tanh(0.5 * x) + 0.5


def _silu(x):
    return x * _sigmoid(x)


def _rms_norm(x, g):
    return x * lax.rsqrt(jnp.mean(x * x, axis=-1, keepdims=True) + EPS) * g


def _dot(a, b):
    return jnp.dot(a, b, preferred_element_type=F32)


def _resident(shape):
    zeros = (0,) * len(shape)
    return pl.BlockSpec(shape, lambda *_: zeros, pipeline_mode=pl.Buffered(1))


def _params(*semantics, flags=None):
    return pltpu.CompilerParams(dimension_semantics=semantics, vmem_limit_bytes=VMEM_LIMIT_BYTES, flags=flags)


def _mod_kernel(c_ref, w_ref, b_ref, o_ref):
    c = c_ref[...]
    o_ref[...] = _dot(_silu(c).astype(BF16), w_ref[...].astype(BF16)) + b_ref[...]


def _mod(c, w_mod, b_mod, tn=1024):
    B, D = c.shape
    N = w_mod.shape[1]
    return pl.pallas_call(
        _mod_kernel,
        grid=(N // tn,),
        in_specs=[_resident((B, D)),
                  pl.BlockSpec((D, tn), lambda j: (0, j)),
                  pl.BlockSpec((1, tn), lambda j: (0, j))],
        out_specs=pl.BlockSpec((B, tn), lambda j: (0, j)),
        out_shape=jax.ShapeDtypeStruct((B, N), F32),
        compiler_params=_params("parallel"),
        name="mod",
    )(c, w_mod, b_mod.reshape(1, N))


def _mod_slices(mod_ref, sub, D):
    base = 3 * sub * D
    return (mod_ref[:, base:base + D], mod_ref[:, base + D:base + 2 * D],
            mod_ref[:, base + 2 * D:base + 3 * D])


def _ffn_kernel(x_ref, mod_ref, g_ref, wgu_ref, wd_ref, gf_ref, o_ref, *, sub, ck, final_norm):
    D = x_ref.shape[-1]
    Fh = wd_ref.shape[0]
    x = x_ref[...]
    shift, scale, gate = _mod_slices(mod_ref, sub, D)
    xb = (_rms_norm(x, g_ref[...]) * (1.0 + scale) + shift).astype(BF16)
    acc = jnp.zeros(x.shape, F32)
    for j in range(Fh // ck):
        g = _dot(xb, wgu_ref[:, j * ck:(j + 1) * ck])
        u = _dot(xb, wgu_ref[:, Fh + j * ck:Fh + (j + 1) * ck])
        acc = acc + _dot((_silu(g) * u).astype(BF16), wd_ref[j * ck:(j + 1) * ck, :])
    h = x + (0.5 * gate) * acc
    if final_norm:
        h = _rms_norm(h, gf_ref[...])
    o_ref[...] = h


def _ffn(x, mod3, g_norm, w_gu, w_d, g_final, *, sub, final_norm, tm=512, ck=256):
    B, S, D = x.shape
    Fh = w_d.shape[0]
    nt = S // tm
    kern = functools.partial(_ffn_kernel, sub=sub, ck=ck, final_norm=final_norm)
    return pl.pallas_call(
        kern,
        grid=(B, nt),
        in_specs=[pl.BlockSpec((None, tm, D), lambda b, i: (b, i, 0)),
                  pl.BlockSpec((None, 1, mod3.shape[-1]), lambda b, i: (b, 0, 0)),
                  _resident((1, D)),
                  _resident((D, 2 * Fh)),
                  _resident((Fh, D)),
                  _resident((1, D))],
        out_specs=pl.BlockSpec((None, tm, D), lambda b, i: (b, i, 0)),
        out_shape=jax.ShapeDtypeStruct((B, S, D), F32),
        compiler_params=_params("parallel", "parallel"),
        name=f"ffn{sub}",
    )(x, mod3, g_norm.reshape(1, D), w_gu, w_d, g_final.reshape(1, D))


def _causal_conv_rows(win_ref, wdw_ref, bdw_ref, cv_ref, r0, rb, anchor):
    D = cv_ref.shape[-1]
    off0 = CONV_HALO - (CONV_WIDTH - 1)
    for c0 in range(0, D, LANES):
        wv = win_ref[r0:r0 + rb + CONV_HALO, c0:c0 + LANES]
        acc = jnp.broadcast_to(bdw_ref[:, c0:c0 + LANES] + anchor, (rb, LANES))
        for m in range(SUBLANES):
            rows = rb if m == 0 else rb + SUBLANES
            g = None
            for p in range(m, off0 + CONV_WIDTH, SUBLANES):
                if p < off0:
                    continue
                term = wv[p - m:p - m + rows] * wdw_ref[p - off0:p - off0 + 1, c0:c0 + LANES]
                g = term if g is None else g + term
            acc = acc + g[m:m + rb]
        cv_ref[r0:r0 + rb, c0:c0 + LANES] = acc.astype(cv_ref.dtype)


def _inproj_kernel(x_ref, mod_ref, pos_ref, freq_ref, g_ref, w_ref, wdw_ref, bdw_ref,
                   q_ref, k_ref, v_ref, sg_ref, gr_ref, gc_ref, cv_ref, win_ref, *, k_scale):
    tm, D = x_ref.shape
    dk = D // RET_HEADS
    half = dk // 2

    @pl.when(pl.program_id(1) == 0)
    def _():
        win_ref[0:CONV_HALO, :] = jnp.zeros((CONV_HALO, D), F32)

    x = x_ref[...]
    shift, scale, _ = _mod_slices(mod_ref, 1, D)
    xb = (_rms_norm(x, g_ref[...]) * (1.0 + scale) + shift).astype(BF16)

    def proj(c0):
        return _dot(xb, w_ref[:, c0:c0 + D])

    def anchor_of(z):
        return z[0:1, 0:LANES] * 0.0

    glu_a = proj(6 * D)
    ang = pos_ref[...].astype(F32) * freq_ref[...] + anchor_of(glu_a)
    cos = jnp.cos(ang)
    sin = jnp.sin(ang)
    glu_b = proj(7 * D)
    win_ref[CONV_HALO:CONV_HALO + tm, :] = glu_a * _sigmoid(glu_b)

    def rope_store(z, out_ref, mult):
        for h in range(RET_HEADS):
            x1 = z[:, h * dk:h * dk + half]
            x2 = z[:, h * dk + half:(h + 1) * dk]
            out_ref[:, h * dk:h * dk + half] = ((x1 * cos - x2 * sin) * mult).astype(BF16)
            out_ref[:, h * dk + half:(h + 1) * dk] = ((x1 * sin + x2 * cos) * mult).astype(BF16)

    def store(out_ref, c0, fn):
        def run(z):
            out_ref[:, c0:c0 + D] = fn(z).astype(BF16)
        return run

    epilogues = [
        (2 * D, store(v_ref, 0, lambda z: z)),
        (3 * D, store(v_ref, D, lambda z: z)),
        (4 * D, store(sg_ref, 0, _silu)),
        (5 * D, store(sg_ref, D, _silu)),
        (8 * D, store(gr_ref, 0, _sigmoid)),
        (9 * D, store(gc_ref, 0, _sigmoid)),
        (0, lambda z: rope_store(z, q_ref, 1.0)),
        (D, lambda z: rope_store(z, k_ref, k_scale)),
    ]
    rb = tm // len(epilogues)
    anchor = anchor_of(glu_b)
    for j, (c0, epilogue) in enumerate(epilogues):
        z = proj(c0)
        _causal_conv_rows(win_ref, wdw_ref, bdw_ref, cv_ref, j * rb, rb, anchor)
        epilogue(z)
        anchor = anchor_of(z)

    win_ref[0:CONV_HALO, :] = win_ref[tm:tm + CONV_HALO, :]


def _inproj(h, mod3, pos3, freqs, g_norm, w_in, w_dw, b_dw, tm=512):
    B, S, D = h.shape
    dk = D // RET_HEADS
    nt = S // tm
    row = lambda n: pl.BlockSpec((None, tm, n), lambda b, i: (b, i, 0))
    out = lambda n: jax.ShapeDtypeStruct((B, S, n), BF16)
    kern = functools.partial(_inproj_kernel, k_scale=float(dk) ** -0.5)
    return pl.pallas_call(
        kern,
        grid=(B, nt),
        in_specs=[row(D),
                  pl.BlockSpec((None, 1, mod3.shape[-1]), lambda b, i: (b, 0, 0)),
                  row(1),
                  _resident((1, dk // 2)),
                  _resident((1, D)),
                  _resident(w_in.shape),
                  _resident((CONV_WIDTH, D)), _resident((1, D))],
        out_specs=[row(D), row(D), row(2 * D), row(2 * D), row(D), row(D), row(D)],
        out_shape=[out(D), out(D), out(2 * D), out(2 * D), out(D), out(D), out(D)],
        scratch_shapes=[pltpu.VMEM((CONV_HALO + tm, D), F32)],
        compiler_params=_params("parallel", "arbitrary"),
        name="inproj",
    )(h, mod3, pos3, freqs, g_norm.reshape(1, D), w_in, w_dw, b_dw.reshape(1, D))


def _retention_kernel(q_ref, k_ref, v_ref, sg_ref, intra_ref, xi_ref, zeta_ref, gng_ref, gnb_ref,
                      o_ref, state_ref, *, gamma_c):
    dk = q_ref.shape[-1] // RET_HEADS
    dv = v_ref.shape[-1] // RET_HEADS

    @pl.when(pl.program_id(1) == 0)
    def _():
        state_ref[...] = jnp.zeros(state_ref.shape, F32)

    for h in range(RET_HEADS):
        qh = q_ref[:, h * dk:(h + 1) * dk]
        kh = k_ref[:, h * dk:(h + 1) * dk]
        vh = v_ref[:, h * dv:(h + 1) * dv]
        s = lax.dot_general(qh, kh, (((1,), (1,)), ((), ())), preferred_element_type=F32) * intra_ref[h]
        st = state_ref[h]
        o = _dot(s.astype(BF16), vh) + _dot(qh, st.astype(BF16)) * xi_ref[h]
        kz = (kh.astype(F32) * zeta_ref[h]).astype(BF16)
        state_ref[h] = st * gamma_c[h] + lax.dot_general(
            kz, vh, (((0,), (0,)), ((), ())), preferred_element_type=F32)
        mu = jnp.mean(o, axis=-1, keepdims=True)
        d = o - mu
        var = jnp.mean(d * d, axis=-1, keepdims=True)
        y = d * lax.rsqrt(var + EPS) * gng_ref[:, h * dv:(h + 1) * dv] + gnb_ref[:, h * dv:(h + 1) * dv]
        o_ref[:, h * dv:(h + 1) * dv] = (sg_ref[:, h * dv:(h + 1) * dv].astype(F32) * y).astype(BF16)


def _retention_tables(chunk):
    log_gamma = np.log1p(-np.power(2.0, -5.0 - np.arange(RET_HEADS, dtype=np.float64)))
    idx = np.arange(chunk, dtype=np.float64)
    diff = idx[:, None] - idx[None, :]
    intra = np.where(diff[None] >= 0, np.exp(log_gamma[:, None, None] * np.maximum(diff, 0.0)[None]), 0.0)
    xi = np.exp(log_gamma[:, None] * (idx + 1.0))[:, :, None]
    zeta = np.exp(log_gamma[:, None] * (chunk - 1.0 - idx))[:, :, None]
    gamma_c = tuple(float(g) for g in np.exp(log_gamma * chunk))
    return (jnp.asarray(intra, F32), jnp.asarray(xi, F32), jnp.asarray(zeta, F32), gamma_c)


def _retention(q, k, v, sg, gn_g, gn_b, chunk=RET_CHUNK):
    B, S, QK = q.shape
    V = v.shape[-1]
    dk, dv = QK // RET_HEADS, V // RET_HEADS
    intra, xi, zeta, gamma_c = _retention_tables(chunk)
    row = lambda n: pl.BlockSpec((None, chunk, n), lambda b, i: (b, i, 0))
    kern = functools.partial(_retention_kernel, gamma_c=gamma_c)
    return pl.pallas_call(
        kern,
        grid=(B, S // chunk),
        in_specs=[row(QK), row(QK), row(V), row(V),
                  _resident(intra.shape), _resident(xi.shape), _resident(zeta.shape),
                  _resident((1, V)), _resident((1, V))],
        out_specs=row(V),
        out_shape=jax.ShapeDtypeStruct((B, S, V), BF16),
        scratch_shapes=[pltpu.VMEM((RET_HEADS, dk, dv), F32)],
        compiler_params=_params("parallel", "arbitrary"),
        name="retention",
    )(q, k, v, sg, intra, xi, zeta, gn_g.reshape(1, V), gn_b.reshape(1, V))


def _merge_kernel(h_ref, mod_ref, og_ref, gr_ref, gc_ref, cv_ref, lng_ref, lnb_ref,
                  wro_ref, wco_ref, bco_ref, wout_ref, o_ref):
    D = h_ref.shape[-1]
    _, _, gate = _mod_slices(mod_ref, 1, D)
    cv = cv_ref[...].astype(F32)
    mu = jnp.mean(cv, axis=-1, keepdims=True)
    d = cv - mu
    var = jnp.mean(d * d, axis=-1, keepdims=True)
    ln = d * lax.rsqrt(var + EPS) * lng_ref[...] + lnb_ref[...]
    y_conv = _dot(_silu(ln).astype(BF16), wco_ref[...]) + bco_ref[...]
    y_ret = _dot(og_ref[...], wro_ref[...])
    merged = gr_ref[...].astype(F32) * y_ret + gc_ref[...].astype(F32) * y_conv
    o_ref[...] = h_ref[...] + gate * _dot(merged.astype(BF16), wout_ref[...])


def _merge(h, mod3, og, gr, gc, cv, ln_g, ln_b, w_ret_o, w_conv_o, b_conv_o, w_out, tm=512):
    B, S, D = h.shape
    V = og.shape[-1]
    row = lambda n: pl.BlockSpec((None, tm, n), lambda b, i: (b, i, 0))
    return pl.pallas_call(
        _merge_kernel,
        grid=(B, S // tm),
        in_specs=[row(D),
                  pl.BlockSpec((None, 1, mod3.shape[-1]), lambda b, i: (b, 0, 0)),
                  row(V), row(D), row(D), row(D),
                  _resident((1, D)), _resident((1, D)),
                  _resident((V, D)), _resident((D, D)), _resident((1, D)), _resident((D, D))],
        out_specs=row(D),
        out_shape=jax.ShapeDtypeStruct((B, S, D), F32),
        compiler_params=_params("parallel", "parallel"),
        name="merge",
    )(h, mod3, og, gr, gc, cv, ln_g.reshape(1, D), ln_b.reshape(1, D), w_ret_o, w_conv_o,
      b_conv_o.reshape(1, D), w_out)


def kernel(x, c, positions, w_mod, b_mod, g_norm1, w_ffn1_gu, w_ffn1_d, g_norm2, w_in, ret_gn_g, ret_gn_b,
           w_ret_o, w_dw, b_dw, conv_ln_g, conv_ln_b, w_conv_o, b_conv_o, w_out, g_norm3, w_ffn2_gu,
           w_ffn2_d, g_normf):
    B, S, D = x.shape
    depth = w_mod.shape[0]
    dk = D // RET_HEADS
    freqs = jnp.asarray(
        np.power(ROPE_BASE, -np.arange(0, dk // 2, dtype=np.float64) * (2.0 / dk)), F32).reshape(1, dk // 2)
    pos3 = positions.reshape(B, S, 1)
    bf = lambda w: w.astype(BF16)

    h = x
    for l in range(depth):
        last = l == depth - 1
        mod3 = _mod(c, w_mod[l], b_mod[l]).reshape(B, 1, 3 * N_SUB * D)
        h = _ffn(h, mod3, g_norm1[l], bf(w_ffn1_gu[l]), bf(w_ffn1_d[l]), g_normf, sub=0, final_norm=False)
        q, k, v, sg, gr, gc, cv = _inproj(h, mod3, pos3, freqs, g_norm2[l], bf(w_in[l]),
                                          w_dw[l].reshape(CONV_WIDTH, D), b_dw[l])
        og = _retention(q, k, v, sg, ret_gn_g[l], ret_gn_b[l])
        h = _merge(h, mod3, og, gr, gc, cv, conv_ln_g[l], conv_ln_b[l], bf(w_ret_o[l]), bf(w_conv_o[l]),
                   b_conv_o[l], bf(w_out[l]))
        h = _ffn(h, mod3, g_norm3[l], bf(w_ffn2_gu[l]), bf(w_ffn2_d[l]), g_normf, sub=2, final_norm=last)
    return h
```

```python
import functools

import numpy as np
import jax
import jax.numpy as jnp
from jax import lax
from jax.experimental import pallas as pl
from jax.experimental.pallas import tpu as pltpu

F32 = jnp.float32
BF16 = jnp.bfloat16

EPS = 1e-6
N_SUB = 3
RET_HEADS = 4
CONV_WIDTH = 31
ROPE_BASE = 10000.0
RET_CHUNK = 256
CONV_HALO = 32
LANES = 128
SUBLANES = 8
VMEM_LIMIT_BYTES = 56 * 1024 * 1024


def _sigmoid(x):
    return 0.5 * jnp.tanh(0.5 * x) + 0.5


def _silu(x):
    h = 0.5 * x
    return h + h * jnp.tanh(h)


def _rms_norm(x, g):
    return x * lax.rsqrt(jnp.mean(x * x, axis=-1, keepdims=True) + EPS) * g


def _dot(a, b):
    return jnp.dot(a, b, preferred_element_type=F32)


def _resident(shape):
    zeros = (0,) * len(shape)
    return pl.BlockSpec(shape, lambda *_: zeros, pipeline_mode=pl.Buffered(1))


def _params(*semantics, flags=None):
    return pltpu.CompilerParams(dimension_semantics=semantics, vmem_limit_bytes=VMEM_LIMIT_BYTES, flags=flags)


def _mod_kernel(c_ref, w_ref, b_ref, o_ref):
    c = c_ref[...]
    o_ref[...] = _dot(_silu(c).astype(BF16), w_ref[...].astype(BF16)) + b_ref[...]


def _mod(c, w_mod, b_mod, tn=1024):
    B, D = c.shape
    N = w_mod.shape[1]
    return pl.pallas_call(
        _mod_kernel,
        grid=(N // tn,),
        in_specs=[_resident((B, D)),
                  pl.BlockSpec((D, tn), lambda j: (0, j)),
                  pl.BlockSpec((1, tn), lambda j: (0, j))],
        out_specs=pl.BlockSpec((B, tn), lambda j: (0, j)),
        out_shape=jax.ShapeDtypeStruct((B, N), F32),
        compiler_params=_params("parallel"),
        name="mod",
    )(c, w_mod, b_mod.reshape(1, N))


def _mod_slices(mod_ref, sub, D):
    base = 3 * sub * D
    return (mod_ref[:, base:base + D], mod_ref[:, base + D:base + 2 * D],
            mod_ref[:, base + 2 * D:base + 3 * D])


def _ffn_kernel(x_ref, mod_ref, g_ref, wgu_ref, wd_ref, gf_ref, o_ref, *, sub, ck, final_norm):
    D = x_ref.shape[-1]
    Fh = wd_ref.shape[0]
    x = x_ref[...]
    shift, scale, gate = _mod_slices(mod_ref, sub, D)
    xb = (_rms_norm(x, g_ref[...]) * (1.0 + scale) + shift).astype(BF16)
    acc = jnp.zeros(x.shape, F32)
    for j in range(Fh // ck):
        g = _dot(xb, wgu_ref[:, j * ck:(j + 1) * ck])
        u = _dot(xb, wgu_ref[:, Fh + j * ck:Fh + (j + 1) * ck])
        acc = acc + _dot((_silu(g) * u).astype(BF16), wd_ref[j * ck:(j + 1) * ck, :])
    h = x + (0.5 * gate) * acc
    if final_norm:
        h = _rms_norm(h, gf_ref[...])
    o_ref[...] = h


def _ffn(x, mod3, g_norm, w_gu, w_d, g_final, *, sub, final_norm, tm=512, ck=256):
    B, S, D = x.shape
    Fh = w_d.shape[0]
    nt = S // tm
    kern = functools.partial(_ffn_kernel, sub=sub, ck=ck, final_norm=final_norm)
    return pl.pallas_call(
        kern,
        grid=(B, nt),
        in_specs=[pl.BlockSpec((None, tm, D), lambda b, i: (b, i, 0)),
                  pl.BlockSpec((None, 1, mod3.shape[-1]), lambda b, i: (b, 0, 0)),
                  _resident((1, D)),
                  _resident((D, 2 * Fh)),
                  _resident((Fh, D)),
                  _resident((1, D))],
        out_specs=pl.BlockSpec((None, tm, D), lambda b, i: (b, i, 0)),
        out_shape=jax.ShapeDtypeStruct((B, S, D), F32),
        compiler_params=_params("parallel", "parallel"),
        name=f"ffn{sub}",
    )(x, mod3, g_norm.reshape(1, D), w_gu, w_d, g_final.reshape(1, D))


def _causal_conv_rows(win_ref, wdw_ref, bdw_ref, cv_ref, r0, rb, anchor):
    D = cv_ref.shape[-1]
    off0 = CONV_HALO - (CONV_WIDTH - 1)
    for c0 in range(0, D, LANES):
        wv = win_ref[r0:r0 + rb + CONV_HALO, c0:c0 + LANES]
        acc = jnp.broadcast_to(bdw_ref[:, c0:c0 + LANES] + anchor, (rb, LANES))
        for m in range(SUBLANES):
            rows = rb if m == 0 else rb + SUBLANES
            g = None
            for p in range(m, off0 + CONV_WIDTH, SUBLANES):
                if p < off0:
                    continue
                term = wv[p - m:p - m + rows] * wdw_ref[p - off0:p - off0 + 1, c0:c0 + LANES]
                g = term if g is None else g + term
            acc = acc + g[m:m + rb]
        cv_ref[r0:r0 + rb, c0:c0 + LANES] = acc.astype(cv_ref.dtype)


def _inproj_kernel(x_ref, mod_ref, pos_ref, freq_ref, g_ref, w_ref, wdw_ref, bdw_ref,
                   q_ref, k_ref, v_ref, gret_ref, gr_ref, gc_ref, cv_ref, win_ref, *, k_scale):
    tm, D = x_ref.shape
    dk = D // RET_HEADS
    half = dk // 2

    @pl.when(pl.program_id(1) == 0)
    def _():
        win_ref[0:CONV_HALO, :] = jnp.zeros((CONV_HALO, D), F32)

    x = x_ref[...]
    shift, scale, _ = _mod_slices(mod_ref, 1, D)
    xb = (_rms_norm(x, g_ref[...]) * (1.0 + scale) + shift).astype(BF16)

    def proj(c0):
        return _dot(xb, w_ref[:, c0:c0 + D])

    def anchor_of(z):
        return z[0:1, 0:LANES] * 0.0

    glu_a = proj(6 * D)
    ang = pos_ref[...].astype(F32) * freq_ref[...] + anchor_of(glu_a)
    cos = jnp.cos(ang)
    sin = jnp.sin(ang)
    glu_b = proj(7 * D)
    win_ref[CONV_HALO:CONV_HALO + tm, :] = glu_a * _sigmoid(glu_b)

    def rope_store(z, out_ref, mult):
        for h in range(RET_HEADS):
            x1 = z[:, h * dk:h * dk + half]
            x2 = z[:, h * dk + half:(h + 1) * dk]
            out_ref[:, h * dk:h * dk + half] = ((x1 * cos - x2 * sin) * mult).astype(BF16)
            out_ref[:, h * dk + half:(h + 1) * dk] = ((x1 * sin + x2 * cos) * mult).astype(BF16)

    def store(out_ref, c0):
        def run(z):
            out_ref[:, c0:c0 + D] = z.astype(BF16)
        return run

    epilogues = [
        (2 * D, store(v_ref, 0)),
        (3 * D, store(v_ref, D)),
        (4 * D, store(gret_ref, 0)),
        (5 * D, store(gret_ref, D)),
        (8 * D, store(gr_ref, 0)),
        (9 * D, store(gc_ref, 0)),
        (0, lambda z: rope_store(z, q_ref, 1.0)),
        (D, lambda z: rope_store(z, k_ref, k_scale)),
    ]
    rb = tm // len(epilogues)
    anchor = anchor_of(glu_b)
    for j, (c0, epilogue) in enumerate(epilogues):
        z = proj(c0)
        _causal_conv_rows(win_ref, wdw_ref, bdw_ref, cv_ref, j * rb, rb, anchor)
        epilogue(z)
        anchor = anchor_of(z)

    win_ref[0:CONV_HALO, :] = win_ref[tm:tm + CONV_HALO, :]


def _inproj(h, mod3, pos3, freqs, g_norm, w_in, w_dw, b_dw, tm=512):
    B, S, D = h.shape
    dk = D // RET_HEADS
    nt = S // tm
    row = lambda n: pl.BlockSpec((None, tm, n), lambda b, i: (b, i, 0))
    out = lambda n: jax.ShapeDtypeStruct((B, S, n), BF16)
    kern = functools.partial(_inproj_kernel, k_scale=float(dk) ** -0.5)
    return pl.pallas_call(
        kern,
        grid=(B, nt),
        in_specs=[row(D),
                  pl.BlockSpec((None, 1, mod3.shape[-1]), lambda b, i: (b, 0, 0)),
                  row(1),
                  _resident((1, dk // 2)),
                  _resident((1, D)),
                  _resident(w_in.shape),
                  _resident((CONV_WIDTH, D)), _resident((1, D))],
        out_specs=[row(D), row(D), row(2 * D), row(2 * D), row(D), row(D), row(D)],
        out_shape=[out(D), out(D), out(2 * D), out(2 * D), out(D), out(D), out(D)],
        scratch_shapes=[pltpu.VMEM((CONV_HALO + tm, D), F32)],
        compiler_params=_params("parallel", "arbitrary"),
        name="inproj",
    )(h, mod3, pos3, freqs, g_norm.reshape(1, D), w_in, w_dw, b_dw.reshape(1, D))


def _retention_kernel(q_ref, k_ref, v_ref, intra_ref, xi_ref, zeta_ref, o_ref, state_ref, *, gamma_c):
    dk = q_ref.shape[-1] // RET_HEADS
    dv = v_ref.shape[-1] // RET_HEADS

    @pl.when(pl.program_id(1) == 0)
    def _():
        state_ref[...] = jnp.zeros(state_ref.shape, F32)

    for h in range(RET_HEADS):
        qh = q_ref[:, h * dk:(h + 1) * dk]
        kh = k_ref[:, h * dk:(h + 1) * dk]
        vh = v_ref[:, h * dv:(h + 1) * dv]
        s = lax.dot_general(qh, kh, (((1,), (1,)), ((), ())), preferred_element_type=F32) * intra_ref[h]
        st = state_ref[h]
        o = _dot(s.astype(BF16), vh) + _dot(qh, st.astype(BF16)) * xi_ref[h]
        o_ref[:, h * dv:(h + 1) * dv] = o.astype(o_ref.dtype)
        kz = (kh.astype(F32) * zeta_ref[h]).astype(BF16)
        state_ref[h] = st * gamma_c[h] + lax.dot_general(
            kz, vh, (((0,), (0,)), ((), ())), preferred_element_type=F32)


def _retention_tables(chunk):
    log_gamma = np.log1p(-np.power(2.0, -5.0 - np.arange(RET_HEADS, dtype=np.float64)))
    idx = np.arange(chunk, dtype=np.float64)
    diff = idx[:, None] - idx[None, :]
    intra = np.where(diff[None] >= 0, np.exp(log_gamma[:, None, None] * np.maximum(diff, 0.0)[None]), 0.0)
    xi = np.exp(log_gamma[:, None] * (idx + 1.0))[:, :, None]
    zeta = np.exp(log_gamma[:, None] * (chunk - 1.0 - idx))[:, :, None]
    gamma_c = tuple(float(g) for g in np.exp(log_gamma * chunk))
    return (jnp.asarray(intra, F32), jnp.asarray(xi, F32), jnp.asarray(zeta, F32), gamma_c)


def _retention(q, k, v, chunk=RET_CHUNK):
    B, S, QK = q.shape
    V = v.shape[-1]
    dk, dv = QK // RET_HEADS, V // RET_HEADS
    intra, xi, zeta, gamma_c = _retention_tables(chunk)
    row = lambda n: pl.BlockSpec((None, chunk, n), lambda b, i: (b, i, 0))
    kern = functools.partial(_retention_kernel, gamma_c=gamma_c)
    return pl.pallas_call(
        kern,
        grid=(B, S // chunk),
        in_specs=[row(QK), row(QK), row(V),
                  _resident(intra.shape), _resident(xi.shape), _resident(zeta.shape)],
        out_specs=row(V),
        out_shape=jax.ShapeDtypeStruct((B, S, V), BF16),
        scratch_shapes=[pltpu.VMEM((RET_HEADS, dk, dv), F32)],
        compiler_params=_params("parallel", "arbitrary"),
        name="retention",
    )(q, k, v, intra, xi, zeta)


def _merge_kernel(h_ref, mod_ref, o_ref_in, gret_ref, gr_ref, gc_ref, cv_ref, gng_ref, gnb_ref,
                  lng_ref, lnb_ref, wro_ref, wco_ref, bco_ref, wout_ref, out_ref):
    D = h_ref.shape[-1]
    dv = o_ref_in.shape[-1] // RET_HEADS
    _, _, gate = _mod_slices(mod_ref, 1, D)

    cv = cv_ref[...].astype(F32)
    mu = jnp.mean(cv, axis=-1, keepdims=True)
    d = cv - mu
    var = jnp.mean(d * d, axis=-1, keepdims=True)
    ln = d * lax.rsqrt(var + EPS) * lng_ref[...] + lnb_ref[...]
    y_conv = _dot(_silu(ln).astype(BF16), wco_ref[...]) + bco_ref[...]

    y_ret = jnp.zeros(h_ref.shape, F32)
    for h in range(RET_HEADS):
        cols = slice(h * dv, (h + 1) * dv)
        o = o_ref_in[:, cols].astype(F32)
        mu = jnp.mean(o, axis=-1, keepdims=True)
        d = o - mu
        var = jnp.mean(d * d, axis=-1, keepdims=True)
        y = d * lax.rsqrt(var + EPS) * gng_ref[:, cols] + gnb_ref[:, cols]
        og = (_silu(gret_ref[:, cols].astype(F32)) * y).astype(BF16)
        y_ret = y_ret + _dot(og, wro_ref[cols, :])

    merged = _sigmoid(gr_ref[...].astype(F32)) * y_ret + _sigmoid(gc_ref[...].astype(F32)) * y_conv
    out_ref[...] = h_ref[...] + gate * _dot(merged.astype(BF16), wout_ref[...])


def _merge(h, mod3, o, gret, gr, gc, cv, gn_g, gn_b, ln_g, ln_b, w_ret_o, w_conv_o, b_conv_o, w_out, tm=512):
    B, S, D = h.shape
    V = o.shape[-1]
    row = lambda n: pl.BlockSpec((None, tm, n), lambda b, i: (b, i, 0))
    return pl.pallas_call(
        _merge_kernel,
        grid=(B, S // tm),
        in_specs=[row(D),
                  pl.BlockSpec((None, 1, mod3.shape[-1]), lambda b, i: (b, 0, 0)),
                  row(V), row(V), row(D), row(D), row(D),
                  _resident((1, V)), _resident((1, V)), _resident((1, D)), _resident((1, D)),
                  _resident((V, D)), _resident((D, D)), _resident((1, D)), _resident((D, D))],
        out_specs=row(D),
        out_shape=jax.ShapeDtypeStruct((B, S, D), F32),
        compiler_params=_params("parallel", "parallel"),
        name="merge",
    )(h, mod3, o, gret, gr, gc, cv, gn_g.reshape(1, V), gn_b.reshape(1, V), ln_g.reshape(1, D),
      ln_b.reshape(1, D), w_ret_o, w_conv_o, b_conv_o.reshape(1, D), w_out)


def kernel(x, c, positions, w_mod, b_mod, g_norm1, w_ffn1_gu, w_ffn1_d, g_norm2, w_in, ret_gn_g, ret_gn_b,
           w_ret_o, w_dw, b_dw, conv_ln_g, conv_ln_b, w_conv_o, b_conv_o, w_out, g_norm3, w_ffn2_gu,
           w_ffn2_d, g_normf):
    B, S, D = x.shape
    depth = w_mod.shape[0]
    dk = D // RET_HEADS
    freqs = jnp.asarray(
        np.power(ROPE_BASE, -np.arange(0, dk // 2, dtype=np.float64) * (2.0 / dk)), F32).reshape(1, dk // 2)
    pos3 = positions.reshape(B, S, 1)
    bf = lambda w: w.astype(BF16)

    h = x
    for l in range(depth):
        last = l == depth - 1
        mod3 = _mod(c, w_mod[l], b_mod[l]).reshape(B, 1, 3 * N_SUB * D)
        h = _ffn(h, mod3, g_norm1[l], bf(w_ffn1_gu[l]), bf(w_ffn1_d[l]), g_normf, sub=0, final_norm=False)
        q, k, v, gret, gr, gc, cv = _inproj(h, mod3, pos3, freqs, g_norm2[l], bf(w_in[l]),
                                            w_dw[l].reshape(CONV_WIDTH, D), b_dw[l])
        o = _retention(q, k, v)
        h = _merge(h, mod3, o, gret, gr, gc, cv, ret_gn_g[l], ret_gn_b[l], conv_ln_g[l], conv_ln_b[l],
                   bf(w_ret_o[l]), bf(w_conv_o[l]), b_conv_o[l], bf(w_out[l]))
        h = _ffn(h, mod3, g_norm3[l], bf(w_ffn2_gu[l]), bf(w_ffn2_d[l]), g_normf, sub=2, final_norm=last)
    return h
```

```python
import functools

import numpy as np
import jax
import jax.numpy as jnp
from jax import lax
from jax.experimental import pallas as pl
from jax.experimental.pallas import tpu as pltpu

F32 = jnp.float32
BF16 = jnp.bfloat16

EPS = 1e-6
N_SUB = 3
RET_HEADS = 4
CONV_WIDTH = 31
ROPE_BASE = 10000.0
RET_CHUNK = 256
CONV_HALO = 32
LANES = 128
SUBLANES = 8
ROPE_PIECES = 8
VMEM_LIMIT_BYTES = 56 * 1024 * 1024
TAIL_VMEM_LIMIT_BYTES = 62 * 1024 * 1024


def _sigmoid(x):
    return 0.5 * jnp.tanh(0.5 * x) + 0.5


def _silu(x):
    h = 0.5 * x
    return h + h * jnp.tanh(h)


def _rms_norm(x, g):
    return x * lax.rsqrt(jnp.mean(x * x, axis=-1, keepdims=True) + EPS) * g


def _dot(a, b):
    return jnp.dot(a, b, preferred_element_type=F32)


def _resident(shape):
    zeros = (0,) * len(shape)
    return pl.BlockSpec(shape, lambda *_: zeros, pipeline_mode=pl.Buffered(1))


def _params(*semantics, vmem_limit_bytes=VMEM_LIMIT_BYTES):
    return pltpu.CompilerParams(dimension_semantics=semantics, vmem_limit_bytes=vmem_limit_bytes)


def _mod_kernel(c_ref, w_ref, b_ref, o_ref):
    c = c_ref[...]
    o_ref[...] = _dot(_silu(c).astype(BF16), w_ref[...].astype(BF16)) + b_ref[...]


def _mod(c, w_mod, b_mod, tn=1024):
    B, D = c.shape
    N = w_mod.shape[1]
    return pl.pallas_call(
        _mod_kernel,
        grid=(N // tn,),
        in_specs=[_resident((B, D)),
                  pl.BlockSpec((D, tn), lambda j: (0, j)),
                  pl.BlockSpec((1, tn), lambda j: (0, j))],
        out_specs=pl.BlockSpec((B, tn), lambda j: (0, j)),
        out_shape=jax.ShapeDtypeStruct((B, N), F32),
        compiler_params=_params("parallel"),
        name="mod",
    )(c, w_mod, b_mod.reshape(1, N))


def _mod_slices(mod_ref, sub, D):
    base = 3 * sub * D
    return (mod_ref[:, base:base + D], mod_ref[:, base + D:base + 2 * D],
            mod_ref[:, base + 2 * D:base + 3 * D])


def _zero_of(v):
    return v[0:1, 0:1] * 0.0


def _ffn_math(x, mod_ref, g_ref, wgu_ref, wd_ref, gf_ref, *, sub, ck, final_norm, side=()):
    D = x.shape[-1]
    Fh = wd_ref.shape[0]
    shift, scale, gate = _mod_slices(mod_ref, sub, D)
    xb = (_rms_norm(x, g_ref[...]) * (1.0 + scale) + shift).astype(BF16)
    acc = None
    zeros = []
    for j in range(Fh // ck):
        g = _dot(xb, wgu_ref[:, j * ck:(j + 1) * ck])
        u = _dot(xb, wgu_ref[:, Fh + j * ck:Fh + (j + 1) * ck])
        zeros.append(_zero_of(g))
        part = _dot((_silu(g) * u).astype(BF16), wd_ref[j * ck:(j + 1) * ck, :])
        acc = part if acc is None else acc + part
    h = x + (0.5 * gate) * acc
    if final_norm:
        h = _rms_norm(h, gf_ref[...])
    for piece, zero in zip(side, zeros):
        piece(zero)
    return h


def _ffn_rope_kernel(x_ref, mod_ref, g_ref, wgu_ref, wd_ref, gf_ref, pos_ref, freq_ref,
                     o_ref, cos_ref, sin_ref, *, ck):
    tm = x_ref.shape[0]
    rows = tm // ROPE_PIECES

    def rope_piece(p):
        def run(zero):
            sl = slice(p * rows, (p + 1) * rows)
            ang = pos_ref[sl, :].astype(F32) * freq_ref[...] + zero
            cos = jnp.cos(ang)
            cos_ref[sl, :] = cos
            sin_ref[sl, :] = jnp.sin(ang)
            return _zero_of(cos)
        return run

    o_ref[...] = _ffn_math(x_ref[...], mod_ref, g_ref, wgu_ref, wd_ref, gf_ref, sub=0, ck=ck,
                           final_norm=False, side=[rope_piece(p) for p in range(ROPE_PIECES)])


def _ffn_rope(x, mod3, g_norm, w_gu, w_d, g_final, pos3, freqs, tm=512, ck=256):
    B, S, D = x.shape
    Fh = w_d.shape[0]
    half = freqs.shape[-1]
    row = lambda n: pl.BlockSpec((None, tm, n), lambda b, i: (b, i, 0))
    kern = functools.partial(_ffn_rope_kernel, ck=ck)
    return pl.pallas_call(
        kern,
        grid=(B, S // tm),
        in_specs=[row(D),
                  pl.BlockSpec((None, 1, mod3.shape[-1]), lambda b, i: (b, 0, 0)),
                  _resident((1, D)),
                  _resident((D, 2 * Fh)),
                  _resident((Fh, D)),
                  _resident((1, D)),
                  row(1),
                  _resident((1, half))],
        out_specs=[row(D), row(half), row(half)],
        out_shape=[jax.ShapeDtypeStruct((B, S, D), F32), jax.ShapeDtypeStruct((B, S, half), F32),
                   jax.ShapeDtypeStruct((B, S, half), F32)],
        compiler_params=_params("parallel", "parallel"),
        name="ffn1",
    )(x, mod3, g_norm.reshape(1, D), w_gu, w_d, g_final.reshape(1, D), pos3, freqs)


def _causal_conv_rows(win_ref, wdw_ref, bdw_ref, cv_ref, r0, rb, anchor):
    D = cv_ref.shape[-1]
    off0 = CONV_HALO - (CONV_WIDTH - 1)
    for c0 in range(0, D, LANES):
        wv = win_ref[r0:r0 + rb + CONV_HALO, c0:c0 + LANES]
        acc = jnp.broadcast_to(bdw_ref[:, c0:c0 + LANES] + anchor, (rb, LANES))
        for m in range(SUBLANES):
            rows = rb if m == 0 else rb + SUBLANES
            g = None
            for p in range(m, off0 + CONV_WIDTH, SUBLANES):
                if p < off0:
                    continue
                term = wv[p - m:p - m + rows] * wdw_ref[p - off0:p - off0 + 1, c0:c0 + LANES]
                g = term if g is None else g + term
            acc = acc + g[m:m + rb]
        cv_ref[r0:r0 + rb, c0:c0 + LANES] = acc.astype(cv_ref.dtype)


def _inproj_kernel(x_ref, mod_ref, cos_ref, sin_ref, g_ref, w_ref, wdw_ref, bdw_ref,
                   q_ref, k_ref, v_ref, gret_ref, gr_ref, gc_ref, cv_ref, win_ref, *, k_scale):
    tm, D = x_ref.shape
    dk = D // RET_HEADS
    half = dk // 2

    @pl.when(pl.program_id(1) == 0)
    def _():
        win_ref[0:CONV_HALO, :] = jnp.zeros((CONV_HALO, D), F32)

    x = x_ref[...]
    shift, scale, _ = _mod_slices(mod_ref, 1, D)
    xb = (_rms_norm(x, g_ref[...]) * (1.0 + scale) + shift).astype(BF16)

    def proj(c0):
        return _dot(xb, w_ref[:, c0:c0 + D])

    def anchor_of(z):
        return z[0:1, 0:LANES] * 0.0

    glu_a = proj(6 * D)
    glu_b = proj(7 * D)
    win_ref[CONV_HALO:CONV_HALO + tm, :] = glu_a * _sigmoid(glu_b)

    def rope_store(z, out_ref, mult):
        cos = cos_ref[...]
        sin = sin_ref[...]
        for h in range(RET_HEADS):
            x1 = z[:, h * dk:h * dk + half]
            x2 = z[:, h * dk + half:(h + 1) * dk]
            out_ref[:, h * dk:h * dk + half] = ((x1 * cos - x2 * sin) * mult).astype(BF16)
            out_ref[:, h * dk + half:(h + 1) * dk] = ((x1 * sin + x2 * cos) * mult).astype(BF16)

    def store(out_ref, c0):
        def run(z):
            out_ref[:, c0:c0 + D] = z.astype(BF16)
        return run

    epilogues = [
        (0, lambda z: rope_store(z, q_ref, 1.0)),
        (D, lambda z: rope_store(z, k_ref, k_scale)),
        (2 * D, store(v_ref, 0)),
        (3 * D, store(v_ref, D)),
        (4 * D, store(gret_ref, 0)),
        (5 * D, store(gret_ref, D)),
        (8 * D, store(gr_ref, 0)),
        (9 * D, store(gc_ref, 0)),
    ]
    rb = tm // len(epilogues)
    anchor = anchor_of(glu_b)
    for j, (c0, epilogue) in enumerate(epilogues):
        z = proj(c0)
        _causal_conv_rows(win_ref, wdw_ref, bdw_ref, cv_ref, j * rb, rb, anchor)
        epilogue(z)
        anchor = anchor_of(z)

    win_ref[0:CONV_HALO, :] = win_ref[tm:tm + CONV_HALO, :]


def _inproj(h, mod3, cos, sin, g_norm, w_in, w_dw, b_dw, tm=512):
    B, S, D = h.shape
    dk = D // RET_HEADS
    nt = S // tm
    row = lambda n: pl.BlockSpec((None, tm, n), lambda b, i: (b, i, 0))
    out = lambda n: jax.ShapeDtypeStruct((B, S, n), BF16)
    kern = functools.partial(_inproj_kernel, k_scale=float(dk) ** -0.5)
    return pl.pallas_call(
        kern,
        grid=(B, nt),
        in_specs=[row(D),
                  pl.BlockSpec((None, 1, mod3.shape[-1]), lambda b, i: (b, 0, 0)),
                  row(dk // 2),
                  row(dk // 2),
                  _resident((1, D)),
                  _resident(w_in.shape),
                  _resident((CONV_WIDTH, D)), _resident((1, D))],
        out_specs=[row(D), row(D), row(2 * D), row(2 * D), row(D), row(D), row(D)],
        out_shape=[out(D), out(D), out(2 * D), out(2 * D), out(D), out(D), out(D)],
        scratch_shapes=[pltpu.VMEM((CONV_HALO + tm, D), F32)],
        compiler_params=_params("parallel", "arbitrary"),
        name="inproj",
    )(h, mod3, cos, sin, g_norm.reshape(1, D), w_in, w_dw, b_dw.reshape(1, D))


def _retention_kernel(q_ref, k_ref, v_ref, intra_ref, xi_ref, zeta_ref, o_ref, state_ref, *, gamma_c):
    dk = q_ref.shape[-1] // RET_HEADS
    dv = v_ref.shape[-1] // RET_HEADS

    @pl.when(pl.program_id(1) == 0)
    def _():
        state_ref[...] = jnp.zeros(state_ref.shape, F32)

    for h in range(RET_HEADS):
        qh = q_ref[:, h * dk:(h + 1) * dk]
        kh = k_ref[:, h * dk:(h + 1) * dk]
        vh = v_ref[:, h * dv:(h + 1) * dv]
        s = lax.dot_general(qh, kh, (((1,), (1,)), ((), ())), preferred_element_type=F32) * intra_ref[h]
        st = state_ref[h]
        o = _dot(s.astype(BF16), vh) + _dot(qh, st.astype(BF16)) * xi_ref[h]
        o_ref[:, h * dv:(h + 1) * dv] = o.astype(o_ref.dtype)
        kz = (kh.astype(F32) * zeta_ref[h]).astype(BF16)
        state_ref[h] = st * gamma_c[h] + lax.dot_general(
            kz, vh, (((0,), (0,)), ((), ())), preferred_element_type=F32)


def _retention_tables(chunk):
    log_gamma = np.log1p(-np.power(2.0, -5.0 - np.arange(RET_HEADS, dtype=np.float64)))
    idx = np.arange(chunk, dtype=np.float64)
    diff = idx[:, None] - idx[None, :]
    intra = np.where(diff[None] >= 0, np.exp(log_gamma[:, None, None] * np.maximum(diff, 0.0)[None]), 0.0)
    xi = np.exp(log_gamma[:, None] * (idx + 1.0))[:, :, None]
    zeta = np.exp(log_gamma[:, None] * (chunk - 1.0 - idx))[:, :, None]
    gamma_c = tuple(float(g) for g in np.exp(log_gamma * chunk))
    return (jnp.asarray(intra, F32), jnp.asarray(xi, F32), jnp.asarray(zeta, F32), gamma_c)


def _retention(q, k, v, chunk=RET_CHUNK):
    B, S, QK = q.shape
    V = v.shape[-1]
    dk, dv = QK // RET_HEADS, V // RET_HEADS
    intra, xi, zeta, gamma_c = _retention_tables(chunk)
    row = lambda n: pl.BlockSpec((None, chunk, n), lambda b, i: (b, i, 0))
    kern = functools.partial(_retention_kernel, gamma_c=gamma_c)
    return pl.pallas_call(
        kern,
        grid=(B, S // chunk),
        in_specs=[row(QK), row(QK), row(V),
                  _resident(intra.shape), _resident(xi.shape), _resident(zeta.shape)],
        out_specs=row(V),
        out_shape=jax.ShapeDtypeStruct((B, S, V), BF16),
        scratch_shapes=[pltpu.VMEM((RET_HEADS, dk, dv), F32)],
        compiler_params=_params("parallel", "arbitrary"),
        name="retention",
    )(q, k, v, intra, xi, zeta)


def _merge_pieces(h1_ref, mod_ref, o_ref_in, gret_ref, gr_ref, gc_ref, cv_ref, gng_ref, gnb_ref,
                  lng_ref, lnb_ref, wro_ref, wco_ref, bco_ref, wout_ref, h2_out):
    tm, D = h1_ref.shape
    dv = o_ref_in.shape[-1] // RET_HEADS
    state = {}

    def conv_tail(zero):
        cv = cv_ref[...].astype(F32)
        mu = jnp.mean(cv, axis=-1, keepdims=True)
        d = cv - mu
        var = jnp.mean(d * d, axis=-1, keepdims=True)
        ln = d * lax.rsqrt(var + EPS) * lng_ref[...] + (lnb_ref[...] + zero)
        state["y_conv"] = _dot(_silu(ln).astype(BF16), wco_ref[...]) + bco_ref[...]
        return _zero_of(state["y_conv"])

    def head_tail(h):
        def run(zero):
            cols = slice(h * dv, (h + 1) * dv)
            o = o_ref_in[:, cols].astype(F32)
            mu = jnp.mean(o, axis=-1, keepdims=True)
            d = o - mu
            var = jnp.mean(d * d, axis=-1, keepdims=True)
            y = d * lax.rsqrt(var + EPS) * gng_ref[:, cols] + (gnb_ref[:, cols] + zero)
            og = (_silu(gret_ref[:, cols].astype(F32)) * y).astype(BF16)
            part = _dot(og, wro_ref[cols, :])
            state["y_ret"] = part if h == 0 else state["y_ret"] + part
            return _zero_of(state["y_ret"])
        return run

    def merge(zero):
        _, _, gate = _mod_slices(mod_ref, 0, D)
        merged = (_sigmoid(gr_ref[...].astype(F32) + zero) * state["y_ret"]
                  + _sigmoid(gc_ref[...].astype(F32) + zero) * state["y_conv"])
        h2 = h1_ref[...] + gate * _dot(merged.astype(BF16), wout_ref[...])
        h2_out(h2)
        return _zero_of(h2)

    return [conv_tail] + [head_tail(h) for h in range(RET_HEADS)] + [merge]


def _tail_kernel(h1_ref, modm_ref, o_ref_in, gret_ref, gr_ref, gc_ref, cv_ref, modf_ref,
                 gng_ref, gnb_ref, lng_ref, lnb_ref, wro_ref, wco_ref, bco_ref, wout_ref,
                 g3_ref, wgu_ref, wd_ref, gf_ref, out_ref, h2_ref, *, ck, final_norm):
    t = pl.program_id(0)

    @pl.when(t == 0)
    def _():
        h2_ref[...] = jnp.zeros(h2_ref.shape, F32)

    def h2_out(h2):
        h2_ref[t % 2] = h2

    side = _merge_pieces(h1_ref, modm_ref, o_ref_in, gret_ref, gr_ref, gc_ref, cv_ref, gng_ref, gnb_ref,
                         lng_ref, lnb_ref, wro_ref, wco_ref, bco_ref, wout_ref, h2_out)
    out_ref[...] = _ffn_math(h2_ref[(t + 1) % 2], modf_ref, g3_ref, wgu_ref, wd_ref, gf_ref,
                             sub=0, ck=ck, final_norm=final_norm, side=side)


def _tail(h1, mod3, o, gret, gr, gc, cv, gn_g, gn_b, ln_g, ln_b, w_ret_o, w_conv_o, b_conv_o, w_out,
          g_norm3, w_gu, w_d, g_final, *, final_norm, tm=512, ck=256):
    B, S, D = h1.shape
    V = o.shape[-1]
    Fh = w_d.shape[0]
    ntb = S // tm
    last_tile = B * ntb - 1

    def merged_tile(t):
        m = jnp.minimum(t, last_tile)
        return m // ntb, m % ntb

    def ffn_tile(t):
        f = jnp.maximum(t - 1, 0)
        return f // ntb, f % ntb

    row = lambda n: pl.BlockSpec((None, tm, n), lambda t: (*merged_tile(t), 0))
    mod_merge, mod_ffn = mod3[:, :, 3 * D:6 * D], mod3[:, :, 6 * D:9 * D]
    mod_block = lambda tile: pl.BlockSpec((None, 1, 3 * D), lambda t: (tile(t)[0], 0, 0))
    kern = functools.partial(_tail_kernel, ck=ck, final_norm=final_norm)
    return pl.pallas_call(
        kern,
        grid=(last_tile + 2,),
        in_specs=[row(D), mod_block(merged_tile), row(V), row(V), row(D), row(D), row(D), mod_block(ffn_tile),
                  _resident((1, V)), _resident((1, V)), _resident((1, D)), _resident((1, D)),
                  _resident((V, D)), _resident((D, D)), _resident((1, D)), _resident((D, D)),
                  _resident((1, D)), _resident((D, 2 * Fh)), _resident((Fh, D)), _resident((1, D))],
        out_specs=pl.BlockSpec((None, tm, D), lambda t: (*ffn_tile(t), 0)),
        out_shape=jax.ShapeDtypeStruct((B, S, D), F32),
        scratch_shapes=[pltpu.VMEM((2, tm, D), F32)],
        compiler_params=_params("arbitrary", vmem_limit_bytes=TAIL_VMEM_LIMIT_BYTES),
        name="tail",
    )(h1, mod_merge, o, gret, gr, gc, cv, mod_ffn, gn_g.reshape(1, V), gn_b.reshape(1, V), ln_g.reshape(1, D),
      ln_b.reshape(1, D), w_ret_o, w_conv_o, b_conv_o.reshape(1, D), w_out,
      g_norm3.reshape(1, D), w_gu, w_d, g_final.reshape(1, D))


def kernel(x, c, positions, w_mod, b_mod, g_norm1, w_ffn1_gu, w_ffn1_d, g_norm2, w_in, ret_gn_g, ret_gn_b,
           w_ret_o, w_dw, b_dw, conv_ln_g, conv_ln_b, w_conv_o, b_conv_o, w_out, g_norm3, w_ffn2_gu,
           w_ffn2_d, g_normf):
    B, S, D = x.shape
    depth = w_mod.shape[0]
    dk = D // RET_HEADS
    freqs = jnp.asarray(
        np.power(ROPE_BASE, -np.arange(0, dk // 2, dtype=np.float64) * (2.0 / dk)), F32).reshape(1, dk // 2)
    pos3 = positions.reshape(B, S, 1)
    bf = lambda w: w.astype(BF16)

    h = x
    for l in range(depth):
        last = l == depth - 1
        mod3 = _mod(c, w_mod[l], b_mod[l]).reshape(B, 1, 3 * N_SUB * D)
        h, cos, sin = _ffn_rope(h, mod3, g_norm1[l], bf(w_ffn1_gu[l]), bf(w_ffn1_d[l]), g_normf, pos3, freqs)
        q, k, v, gret, gr, gc, cv = _inproj(h, mod3, cos, sin, g_norm2[l], bf(w_in[l]),
                                            w_dw[l].reshape(CONV_WIDTH, D), b_dw[l])
        o = _retention(q, k, v)
        h = _tail(h, mod3, o, gret, gr, gc, cv, ret_gn_g[l], ret_gn_b[l], conv_ln_g[l], conv_ln_b[l],
                  bf(w_ret_o[l]), bf(w_conv_o[l]), b_conv_o[l], bf(w_out[l]),
                  g_norm3[l], bf(w_ffn2_gu[l]), bf(w_ffn2_d[l]), g_normf, final_norm=last)
    return h
```

```python
import functools

import numpy as np
import jax
import jax.numpy as jnp
from jax import lax
from jax.experimental import pallas as pl
from jax.experimental.pallas import tpu as pltpu

F32 = jnp.float32
BF16 = jnp.bfloat16

EPS = 1e-6
N_SUB = 3
RET_HEADS = 4
CONV_WIDTH = 31
ROPE_BASE = 10000.0
RET_CHUNK = 256
CONV_HALO = 32
LANES = 128
SUBLANES = 8
VMEM_LIMIT_BYTES = 56 * 1024 * 1024
TAIL_VMEM_LIMIT_BYTES = 62 * 1024 * 1024


def _sigmoid(x):
    return 0.5 * jnp.tanh(0.5 * x) + 0.5


def _silu(x):
    h = 0.5 * x
    return h + h * jnp.tanh(h)


def _rms_norm(x, g):
    return x * lax.rsqrt(jnp.mean(x * x, axis=-1, keepdims=True) + EPS) * g


def _dot(a, b):
    return jnp.dot(a, b, preferred_element_type=F32)


def _resident(shape):
    zeros = (0,) * len(shape)
    return pl.BlockSpec(shape, lambda *_: zeros, pipeline_mode=pl.Buffered(1))


def _params(*semantics, vmem_limit_bytes=VMEM_LIMIT_BYTES):
    return pltpu.CompilerParams(dimension_semantics=semantics, vmem_limit_bytes=vmem_limit_bytes)


def _mod_kernel(c_ref, w_ref, b_ref, o_ref):
    c = c_ref[...]
    o_ref[...] = _dot(_silu(c).astype(BF16), w_ref[...].astype(BF16)) + b_ref[...]


def _mod(c, w_mod, b_mod, tn=1024):
    B, D = c.shape
    N = w_mod.shape[1]
    return pl.pallas_call(
        _mod_kernel,
        grid=(N // tn,),
        in_specs=[_resident((B, D)),
                  pl.BlockSpec((D, tn), lambda j: (0, j)),
                  pl.BlockSpec((1, tn), lambda j: (0, j))],
        out_specs=pl.BlockSpec((B, tn), lambda j: (0, j)),
        out_shape=jax.ShapeDtypeStruct((B, N), F32),
        compiler_params=_params("parallel"),
        name="mod",
    )(c, w_mod, b_mod.reshape(1, N))


def _mod_slices(mod_ref, sub, D):
    base = 3 * sub * D
    return (mod_ref[:, base:base + D], mod_ref[:, base + D:base + 2 * D],
            mod_ref[:, base + 2 * D:base + 3 * D])


def _zero_of(v):
    return v[0:1, 0:1] * 0.0


def _ffn_math(x, mod_ref, g_ref, wgu_ref, wd_ref, gf_ref, *, sub, ck, final_norm, side=()):
    D = x.shape[-1]
    Fh = wd_ref.shape[0]
    shift, scale, gate = _mod_slices(mod_ref, sub, D)
    xb = (_rms_norm(x, g_ref[...]) * (1.0 + scale) + shift).astype(BF16)
    acc = None
    zeros = []
    for j in range(Fh // ck):
        g = _dot(xb, wgu_ref[:, j * ck:(j + 1) * ck])
        u = _dot(xb, wgu_ref[:, Fh + j * ck:Fh + (j + 1) * ck])
        zeros.append(_zero_of(g))
        part = _dot((_silu(g) * u).astype(BF16), wd_ref[j * ck:(j + 1) * ck, :])
        acc = part if acc is None else acc + part
    h = x + (0.5 * gate) * acc
    if final_norm:
        h = _rms_norm(h, gf_ref[...])
    for piece, zero in zip(side, zeros):
        piece(zero)
    return h


def _ffn_kernel(x_ref, mod_ref, g_ref, wgu_ref, wd_ref, gf_ref, o_ref, *, sub, ck, final_norm):
    o_ref[...] = _ffn_math(x_ref[...], mod_ref, g_ref, wgu_ref, wd_ref, gf_ref,
                           sub=sub, ck=ck, final_norm=final_norm)


def _ffn(x, mod3, g_norm, w_gu, w_d, g_final, *, sub, final_norm, tm=512, ck=256):
    B, S, D = x.shape
    Fh = w_d.shape[0]
    row = pl.BlockSpec((None, tm, D), lambda b, i: (b, i, 0))
    kern = functools.partial(_ffn_kernel, sub=sub, ck=ck, final_norm=final_norm)
    return pl.pallas_call(
        kern,
        grid=(B, S // tm),
        in_specs=[row,
                  pl.BlockSpec((None, 1, mod3.shape[-1]), lambda b, i: (b, 0, 0)),
                  _resident((1, D)),
                  _resident((D, 2 * Fh)),
                  _resident((Fh, D)),
                  _resident((1, D))],
        out_specs=row,
        out_shape=jax.ShapeDtypeStruct((B, S, D), F32),
        compiler_params=_params("parallel", "parallel"),
        name=f"ffn{sub}",
    )(x, mod3, g_norm.reshape(1, D), w_gu, w_d, g_final.reshape(1, D))


def _causal_conv_rows(win_ref, wdw_ref, bdw_ref, cv_ref, r0, rb, anchor):
    D = cv_ref.shape[-1]
    off0 = CONV_HALO - (CONV_WIDTH - 1)
    for c0 in range(0, D, LANES):
        wv = win_ref[r0:r0 + rb + CONV_HALO, c0:c0 + LANES]
        acc = jnp.broadcast_to(bdw_ref[:, c0:c0 + LANES] + anchor, (rb, LANES))
        for m in range(SUBLANES):
            rows = rb if m == 0 else rb + SUBLANES
            g = None
            for p in range(m, off0 + CONV_WIDTH, SUBLANES):
                if p < off0:
                    continue
                term = wv[p - m:p - m + rows] * wdw_ref[p - off0:p - off0 + 1, c0:c0 + LANES]
                g = term if g is None else g + term
            acc = acc + g[m:m + rb]
        cv_ref[r0:r0 + rb, c0:c0 + LANES] = acc.astype(cv_ref.dtype)


def _rope_tables(contig_ref, pos_ref, freq_ref, ctab_ref, stab_ref, cos_ref, sin_ref):
    @pl.when(contig_ref[0, 0] != 0)
    def _():
        a0 = pos_ref[0:1, :].astype(F32) * freq_ref[...]
        c0, s0 = jnp.cos(a0), jnp.sin(a0)
        ct, st = ctab_ref[...], stab_ref[...]
        cos_ref[...] = c0 * ct - s0 * st
        sin_ref[...] = s0 * ct + c0 * st

    @pl.when(contig_ref[0, 0] == 0)
    def _():
        ang = pos_ref[...].astype(F32) * freq_ref[...]
        cos_ref[...] = jnp.cos(ang)
        sin_ref[...] = jnp.sin(ang)


def _inproj_kernel(contig_ref, x_ref, mod_ref, pos_ref, freq_ref, ctab_ref, stab_ref, g_ref, w_ref, wdw_ref,
                   bdw_ref, q_ref, k_ref, v_ref, gret_ref, gr_ref, gc_ref, cv_ref,
                   win_ref, cos_ref, sin_ref, *, k_scale):
    tm, D = x_ref.shape
    dk = D // RET_HEADS
    half = dk // 2

    @pl.when(pl.program_id(1) == 0)
    def _():
        win_ref[0:CONV_HALO, :] = jnp.zeros((CONV_HALO, D), F32)

    _rope_tables(contig_ref, pos_ref, freq_ref, ctab_ref, stab_ref, cos_ref, sin_ref)

    x = x_ref[...]
    shift, scale, _ = _mod_slices(mod_ref, 1, D)
    xb = (_rms_norm(x, g_ref[...]) * (1.0 + scale) + shift).astype(BF16)

    def proj(c0):
        return _dot(xb, w_ref[:, c0:c0 + D])

    def anchor_of(z):
        return z[0:1, 0:LANES] * 0.0

    glu_a = proj(6 * D)
    glu_b = proj(7 * D)
    win_ref[CONV_HALO:CONV_HALO + tm, :] = glu_a * _sigmoid(glu_b)

    def rope_store(z, out_ref, mult):
        cos = cos_ref[...]
        sin = sin_ref[...]
        for h in range(RET_HEADS):
            x1 = z[:, h * dk:h * dk + half]
            x2 = z[:, h * dk + half:(h + 1) * dk]
            out_ref[:, h * dk:h * dk + half] = ((x1 * cos - x2 * sin) * mult).astype(BF16)
            out_ref[:, h * dk + half:(h + 1) * dk] = ((x1 * sin + x2 * cos) * mult).astype(BF16)

    def store(out_ref, c0):
        def run(z):
            out_ref[:, c0:c0 + D] = z.astype(BF16)
        return run

    epilogues = [
        (0, lambda z: rope_store(z, q_ref, 1.0)),
        (D, lambda z: rope_store(z, k_ref, k_scale)),
        (2 * D, store(v_ref, 0)),
        (3 * D, store(v_ref, D)),
        (4 * D, store(gret_ref, 0)),
        (5 * D, store(gret_ref, D)),
        (8 * D, store(gr_ref, 0)),
        (9 * D, store(gc_ref, 0)),
    ]
    rb = tm // len(epilogues)
    anchor = anchor_of(glu_b)
    for j, (c0, epilogue) in enumerate(epilogues):
        z = proj(c0)
        _causal_conv_rows(win_ref, wdw_ref, bdw_ref, cv_ref, j * rb, rb, anchor)
        epilogue(z)
        anchor = anchor_of(z)

    win_ref[0:CONV_HALO, :] = win_ref[tm:tm + CONV_HALO, :]


def _inproj(h, mod3, positions, g_norm, w_in, w_dw, b_dw, tm=512):
    B, S, D = h.shape
    dk = D // RET_HEADS
    half = dk // 2
    nt = S // tm
    freqs = np.power(ROPE_BASE, -np.arange(0, half, dtype=np.float64) * (2.0 / dk))
    step = np.arange(tm, dtype=np.float64)[:, None] * freqs[None, :]
    ctab, stab = jnp.asarray(np.cos(step), F32), jnp.asarray(np.sin(step), F32)
    contig = jnp.all(positions[:, 1:] - positions[:, :-1] == 1).astype(jnp.int32).reshape(1, 1)
    row = lambda n: pl.BlockSpec((None, tm, n), lambda b, i: (b, i, 0))
    out = lambda n: jax.ShapeDtypeStruct((B, S, n), BF16)
    kern = functools.partial(_inproj_kernel, k_scale=float(dk) ** -0.5)
    return pl.pallas_call(
        kern,
        grid=(B, nt),
        in_specs=[pl.BlockSpec(memory_space=pltpu.SMEM),
                  row(D),
                  pl.BlockSpec((None, 1, mod3.shape[-1]), lambda b, i: (b, 0, 0)),
                  row(1),
                  _resident((1, half)), _resident((tm, half)), _resident((tm, half)),
                  _resident((1, D)),
                  _resident(w_in.shape),
                  _resident((CONV_WIDTH, D)), _resident((1, D))],
        out_specs=[row(D), row(D), row(2 * D), row(2 * D), row(D), row(D), row(D)],
        out_shape=[out(D), out(D), out(2 * D), out(2 * D), out(D), out(D), out(D)],
        scratch_shapes=[pltpu.VMEM((CONV_HALO + tm, D), F32), pltpu.VMEM((tm, half), F32),
                        pltpu.VMEM((tm, half), F32)],
        compiler_params=_params("parallel", "arbitrary"),
        name="inproj",
    )(contig, h, mod3, positions.reshape(B, S, 1), jnp.asarray(freqs, F32).reshape(1, half), ctab, stab,
      g_norm.reshape(1, D), w_in, w_dw, b_dw.reshape(1, D))


def _retention_kernel(q_ref, k_ref, v_ref, intra_ref, xi_ref, zeta_ref, o_ref, state_ref, *, gamma_c):
    dk = q_ref.shape[-1] // RET_HEADS
    dv = v_ref.shape[-1] // RET_HEADS

    @pl.when(pl.program_id(1) == 0)
    def _():
        state_ref[...] = jnp.zeros(state_ref.shape, F32)

    for h in range(RET_HEADS):
        qh = q_ref[:, h * dk:(h + 1) * dk]
        kh = k_ref[:, h * dk:(h + 1) * dk]
        vh = v_ref[:, h * dv:(h + 1) * dv]
        s = lax.dot_general(qh, kh, (((1,), (1,)), ((), ())), preferred_element_type=F32) * intra_ref[h]
        st = state_ref[h]
        o = _dot(s.astype(BF16), vh) + _dot(qh, st.astype(BF16)) * xi_ref[h]
        o_ref[:, h * dv:(h + 1) * dv] = o.astype(o_ref.dtype)
        kz = (kh.astype(F32) * zeta_ref[h]).astype(BF16)
        state_ref[h] = st * gamma_c[h] + lax.dot_general(
            kz, vh, (((0,), (0,)), ((), ())), preferred_element_type=F32)


def _retention_tables(chunk):
    log_gamma = np.log1p(-np.power(2.0, -5.0 - np.arange(RET_HEADS, dtype=np.float64)))
    idx = np.arange(chunk, dtype=np.float64)
    diff = idx[:, None] - idx[None, :]
    intra = np.where(diff[None] >= 0, np.exp(log_gamma[:, None, None] * np.maximum(diff, 0.0)[None]), 0.0)
    xi = np.exp(log_gamma[:, None] * (idx + 1.0))[:, :, None]
    zeta = np.exp(log_gamma[:, None] * (chunk - 1.0 - idx))[:, :, None]
    gamma_c = tuple(float(g) for g in np.exp(log_gamma * chunk))
    return (jnp.asarray(intra, F32), jnp.asarray(xi, F32), jnp.asarray(zeta, F32), gamma_c)


def _retention(q, k, v, chunk=RET_CHUNK):
    B, S, QK = q.shape
    V = v.shape[-1]
    dk, dv = QK // RET_HEADS, V // RET_HEADS
    intra, xi, zeta, gamma_c = _retention_tables(chunk)
    row = lambda n: pl.BlockSpec((None, chunk, n), lambda b, i: (b, i, 0))
    kern = functools.partial(_retention_kernel, gamma_c=gamma_c)
    return pl.pallas_call(
        kern,
        grid=(B, S // chunk),
        in_specs=[row(QK), row(QK), row(V),
                  _resident(intra.shape), _resident(xi.shape), _resident(zeta.shape)],
        out_specs=row(V),
        out_shape=jax.ShapeDtypeStruct((B, S, V), BF16),
        scratch_shapes=[pltpu.VMEM((RET_HEADS, dk, dv), F32)],
        compiler_params=_params("parallel", "arbitrary"),
        name="retention",
    )(q, k, v, intra, xi, zeta)


def _merge_pieces(h1_ref, mod_ref, o_ref_in, gret_ref, gr_ref, gc_ref, cv_ref, gng_ref, gnb_ref,
                  lng_ref, lnb_ref, wro_ref, wco_ref, bco_ref, wout_ref, h2_out):
    D = h1_ref.shape[-1]
    dv = o_ref_in.shape[-1] // RET_HEADS
    state = {}

    def conv_tail(zero):
        cv = cv_ref[...].astype(F32)
        mu = jnp.mean(cv, axis=-1, keepdims=True)
        d = cv - mu
        var = jnp.mean(d * d, axis=-1, keepdims=True)
        ln = d * lax.rsqrt(var + EPS) * lng_ref[...] + (lnb_ref[...] + zero)
        state["y_conv"] = _dot(_silu(ln).astype(BF16), wco_ref[...]) + bco_ref[...]

    def head_tail(h):
        def run(zero):
            cols = slice(h * dv, (h + 1) * dv)
            o = o_ref_in[:, cols].astype(F32)
            mu = jnp.mean(o, axis=-1, keepdims=True)
            d = o - mu
            var = jnp.mean(d * d, axis=-1, keepdims=True)
            y = d * lax.rsqrt(var + EPS) * gng_ref[:, cols] + (gnb_ref[:, cols] + zero)
            og = (_silu(gret_ref[:, cols].astype(F32)) * y).astype(BF16)
            part = _dot(og, wro_ref[cols, :])
            state["y_ret"] = part if h == 0 else state["y_ret"] + part
        return run

    def merge(zero):
        _, _, gate = _mod_slices(mod_ref, 0, D)
        merged = (_sigmoid(gr_ref[...].astype(F32) + zero) * state["y_ret"]
                  + _sigmoid(gc_ref[...].astype(F32) + zero) * state["y_conv"])
        h2_out(h1_ref[...] + gate * _dot(merged.astype(BF16), wout_ref[...]))

    return [conv_tail] + [head_tail(h) for h in range(RET_HEADS)] + [merge]


def _tail_kernel(h1_ref, modm_ref, o_ref_in, gret_ref, gr_ref, gc_ref, cv_ref, modf_ref,
                 gng_ref, gnb_ref, lng_ref, lnb_ref, wro_ref, wco_ref, bco_ref, wout_ref,
                 g3_ref, wgu_ref, wd_ref, gf_ref, out_ref, h2_ref, *, ck, final_norm):
    t = pl.program_id(0)

    @pl.when(t == 0)
    def _():
        h2_ref[...] = jnp.zeros(h2_ref.shape, F32)

    def h2_out(h2):
        h2_ref[t % 2] = h2

    side = _merge_pieces(h1_ref, modm_ref, o_ref_in, gret_ref, gr_ref, gc_ref, cv_ref, gng_ref, gnb_ref,
                         lng_ref, lnb_ref, wro_ref, wco_ref, bco_ref, wout_ref, h2_out)
    out_ref[...] = _ffn_math(h2_ref[(t + 1) % 2], modf_ref, g3_ref, wgu_ref, wd_ref, gf_ref,
                             sub=0, ck=ck, final_norm=final_norm, side=side)


def _tail(h1, mod3, o, gret, gr, gc, cv, gn_g, gn_b, ln_g, ln_b, w_ret_o, w_conv_o, b_conv_o, w_out,
          g_norm3, w_gu, w_d, g_final, *, final_norm, tm=512, ck=256):
    B, S, D = h1.shape
    V = o.shape[-1]
    Fh = w_d.shape[0]
    ntb = S // tm
    last_tile = B * ntb - 1

    def merged_tile(t):
        m = jnp.minimum(t, last_tile)
        return m // ntb, m % ntb

    def ffn_tile(t):
        f = jnp.maximum(t - 1, 0)
        return f // ntb, f % ntb

    row = lambda n: pl.BlockSpec((None, tm, n), lambda t: (*merged_tile(t), 0))
    mod_merge, mod_ffn = mod3[:, :, 3 * D:6 * D], mod3[:, :, 6 * D:9 * D]
    mod_block = lambda tile: pl.BlockSpec((None, 1, 3 * D), lambda t: (tile(t)[0], 0, 0))
    kern = functools.partial(_tail_kernel, ck=ck, final_norm=final_norm)
    return pl.pallas_call(
        kern,
        grid=(last_tile + 2,),
        in_specs=[row(D), mod_block(merged_tile), row(V), row(V), row(D), row(D), row(D), mod_block(ffn_tile),
                  _resident((1, V)), _resident((1, V)), _resident((1, D)), _resident((1, D)),
                  _resident((V, D)), _resident((D, D)), _resident((1, D)), _resident((D, D)),
                  _resident((1, D)), _resident((D, 2 * Fh)), _resident((Fh, D)), _resident((1, D))],
        out_specs=pl.BlockSpec((None, tm, D), lambda t: (*ffn_tile(t), 0)),
        out_shape=jax.ShapeDtypeStruct((B, S, D), F32),
        scratch_shapes=[pltpu.VMEM((2, tm, D), F32)],
        compiler_params=_params("arbitrary", vmem_limit_bytes=TAIL_VMEM_LIMIT_BYTES),
        name="tail",
    )(h1, mod_merge, o, gret, gr, gc, cv, mod_ffn, gn_g.reshape(1, V), gn_b.reshape(1, V), ln_g.reshape(1, D),
      ln_b.reshape(1, D), w_ret_o, w_conv_o, b_conv_o.reshape(1, D), w_out,
      g_norm3.reshape(1, D), w_gu, w_d, g_final.reshape(1, D))


def kernel(x, c, positions, w_mod, b_mod, g_norm1, w_ffn1_gu, w_ffn1_d, g_norm2, w_in, ret_gn_g, ret_gn_b,
           w_ret_o, w_dw, b_dw, conv_ln_g, conv_ln_b, w_conv_o, b_conv_o, w_out, g_norm3, w_ffn2_gu,
           w_ffn2_d, g_normf):
    B, S, D = x.shape
    depth = w_mod.shape[0]
    bf = lambda w: w.astype(BF16)

    h = x
    for l in range(depth):
        last = l == depth - 1
        mod3 = _mod(c, w_mod[l], b_mod[l]).reshape(B, 1, 3 * N_SUB * D)
        h = _ffn(h, mod3, g_norm1[l], bf(w_ffn1_gu[l]), bf(w_ffn1_d[l]), g_normf, sub=0, final_norm=False)
        q, k, v, gret, gr, gc, cv = _inproj(h, mod3, positions, g_norm2[l], bf(w_in[l]),
                                            w_dw[l].reshape(CONV_WIDTH, D), b_dw[l])
        o = _retention(q, k, v)
        h = _tail(h, mod3, o, gret, gr, gc, cv, ret_gn_g[l], ret_gn_b[l], conv_ln_g[l], conv_ln_b[l],
                  bf(w_ret_o[l]), bf(w_conv_o[l]), b_conv_o[l], bf(w_out[l]),
                  g_norm3[l], bf(w_ffn2_gu[l]), bf(w_ffn2_d[l]), g_normf, final_norm=last)
    return h
```

```python
import functools

import numpy as np
import jax
import jax.numpy as jnp
from jax import lax
from jax.experimental import pallas as pl
from jax.experimental.pallas import tpu as pltpu

F32 = jnp.float32
BF16 = jnp.bfloat16

EPS = 1e-6
N_SUB = 3
RET_HEADS = 4
CONV_WIDTH = 31
ROPE_BASE = 10000.0
RET_CHUNK = 256
CONV_HALO = 32
LANES = 128
SUBLANES = 8
VMEM_LIMIT_BYTES = 56 * 1024 * 1024
TAIL_VMEM_LIMIT_BYTES = 62 * 1024 * 1024


def _sigmoid(x):
    return 0.5 * jnp.tanh(0.5 * x) + 0.5


def _silu(x):
    h = 0.5 * x
    return h + h * jnp.tanh(h)


def _rms_norm(x, g):
    return x * lax.rsqrt(jnp.mean(x * x, axis=-1, keepdims=True) + EPS) * g


def _dot(a, b):
    return jnp.dot(a, b, preferred_element_type=F32)


def _resident(shape):
    zeros = (0,) * len(shape)
    return pl.BlockSpec(shape, lambda *_: zeros, pipeline_mode=pl.Buffered(1))


def _params(*semantics, vmem_limit_bytes=VMEM_LIMIT_BYTES):
    return pltpu.CompilerParams(dimension_semantics=semantics, vmem_limit_bytes=vmem_limit_bytes)


def _mod_kernel(c_ref, w_ref, b_ref, o_ref):
    c = c_ref[...]
    o_ref[...] = _dot(_silu(c).astype(BF16), w_ref[...].astype(BF16)) + b_ref[...]


def _mod(c, w_mod, b_mod, tn=1024):
    B, D = c.shape
    N = w_mod.shape[1]
    return pl.pallas_call(
        _mod_kernel,
        grid=(N // tn,),
        in_specs=[_resident((B, D)),
                  pl.BlockSpec((D, tn), lambda j: (0, j)),
                  pl.BlockSpec((1, tn), lambda j: (0, j))],
        out_specs=pl.BlockSpec((B, tn), lambda j: (0, j)),
        out_shape=jax.ShapeDtypeStruct((B, N), F32),
        compiler_params=_params("parallel"),
        name="mod",
    )(c, w_mod, b_mod.reshape(1, N))


def _mod_slices(mod_ref, sub, D):
    base = 3 * sub * D
    return (mod_ref[:, base:base + D], mod_ref[:, base + D:base + 2 * D],
            mod_ref[:, base + 2 * D:base + 3 * D])


def _zero_of(v):
    return v[0:1, 0:1] * 0.0


def _ffn_math(x, mod_ref, g_ref, wgu_ref, wd_ref, gf_ref, *, sub, ck, final_norm, side=()):
    D = x.shape[-1]
    Fh = wd_ref.shape[0]
    shift, scale, gate = _mod_slices(mod_ref, sub, D)
    xb = (_rms_norm(x, g_ref[...]) * (1.0 + scale) + shift).astype(BF16)
    acc = None
    zeros = []
    for j in range(Fh // ck):
        g = _dot(xb, wgu_ref[:, j * ck:(j + 1) * ck])
        u = _dot(xb, wgu_ref[:, Fh + j * ck:Fh + (j + 1) * ck])
        zeros.append(_zero_of(g))
        part = _dot((_silu(g) * u).astype(BF16), wd_ref[j * ck:(j + 1) * ck, :])
        acc = part if acc is None else acc + part
    h = x + (0.5 * gate) * acc
    if final_norm:
        h = _rms_norm(h, gf_ref[...])
    for piece, zero in zip(side, zeros):
        piece(zero)
    return h


def _ffn_kernel(x_ref, mod_ref, g_ref, wgu_ref, wd_ref, gf_ref, o_ref, *, sub, ck, final_norm):
    o_ref[...] = _ffn_math(x_ref[...], mod_ref, g_ref, wgu_ref, wd_ref, gf_ref,
                           sub=sub, ck=ck, final_norm=final_norm)


def _ffn(x, mod3, g_norm, w_gu, w_d, g_final, *, sub, final_norm, tm=512, ck=256):
    B, S, D = x.shape
    Fh = w_d.shape[0]
    row = pl.BlockSpec((None, tm, D), lambda b, i: (b, i, 0))
    kern = functools.partial(_ffn_kernel, sub=sub, ck=ck, final_norm=final_norm)
    return pl.pallas_call(
        kern,
        grid=(B, S // tm),
        in_specs=[row,
                  pl.BlockSpec((None, 1, mod3.shape[-1]), lambda b, i: (b, 0, 0)),
                  _resident((1, D)),
                  _resident((D, 2 * Fh)),
                  _resident((Fh, D)),
                  _resident((1, D))],
        out_specs=row,
        out_shape=jax.ShapeDtypeStruct((B, S, D), F32),
        compiler_params=_params("parallel", "parallel"),
        name=f"ffn{sub}",
    )(x, mod3, g_norm.reshape(1, D), w_gu, w_d, g_final.reshape(1, D))


def _causal_conv_rows(win_ref, wdw_ref, bdw_ref, cv_ref, r0, rb, anchor):
    D = cv_ref.shape[-1]
    off0 = CONV_HALO - (CONV_WIDTH - 1)
    for c0 in range(0, D, LANES):
        wv = win_ref[c0 // LANES, r0:r0 + rb + CONV_HALO, :]
        acc = jnp.broadcast_to(bdw_ref[:, c0:c0 + LANES] + anchor, (rb, LANES))
        for m in range(SUBLANES):
            rows = rb if m == 0 else rb + SUBLANES
            g = None
            for p in range(m, off0 + CONV_WIDTH, SUBLANES):
                if p < off0:
                    continue
                term = wv[p - m:p - m + rows] * wdw_ref[p - off0:p - off0 + 1, c0:c0 + LANES]
                g = term if g is None else g + term
            acc = acc + g[m:m + rb]
        cv_ref[r0:r0 + rb, c0:c0 + LANES] = acc.astype(cv_ref.dtype)


def _rope_tables(contig_ref, pos_ref, freq_ref, ctab_ref, stab_ref, cos_ref, sin_ref):
    @pl.when(contig_ref[0, 0] != 0)
    def _():
        a0 = pos_ref[0:1, :].astype(F32) * freq_ref[...]
        c0, s0 = jnp.cos(a0), jnp.sin(a0)
        ct, st = ctab_ref[...], stab_ref[...]
        cos_ref[...] = c0 * ct - s0 * st
        sin_ref[...] = s0 * ct + c0 * st

    @pl.when(contig_ref[0, 0] == 0)
    def _():
        ang = pos_ref[...].astype(F32) * freq_ref[...]
        cos_ref[...] = jnp.cos(ang)
        sin_ref[...] = jnp.sin(ang)


def _inproj_kernel(contig_ref, x_ref, mod_ref, pos_ref, freq_ref, ctab_ref, stab_ref, g_ref, w_ref, wdw_ref,
                   bdw_ref, q_ref, k_ref, v_ref, gret_ref, gr_ref, gc_ref, cv_ref,
                   win_ref, cos_ref, sin_ref, *, k_scale):
    tm, D = x_ref.shape
    dk = D // RET_HEADS
    half = dk // 2

    @pl.when(pl.program_id(1) == 0)
    def _():
        win_ref[:, 0:CONV_HALO, :] = jnp.zeros((D // LANES, CONV_HALO, LANES), F32)

    _rope_tables(contig_ref, pos_ref, freq_ref, ctab_ref, stab_ref, cos_ref, sin_ref)

    x = x_ref[...]
    shift, scale, _ = _mod_slices(mod_ref, 1, D)
    xb = (_rms_norm(x, g_ref[...]) * (1.0 + scale) + shift).astype(BF16)

    def proj(c0):
        return _dot(xb, w_ref[:, c0:c0 + D])

    def anchor_of(z):
        return z[0:1, 0:LANES] * 0.0

    glu_a = proj(6 * D)
    glu_b = proj(7 * D)
    u = glu_a * _sigmoid(glu_b)
    for cb in range(D // LANES):
        win_ref[cb, CONV_HALO:CONV_HALO + tm, :] = u[:, cb * LANES:(cb + 1) * LANES]

    def rope_store(z, out_ref, mult):
        cos = cos_ref[...]
        sin = sin_ref[...]
        for h in range(RET_HEADS):
            x1 = z[:, h * dk:h * dk + half]
            x2 = z[:, h * dk + half:(h + 1) * dk]
            out_ref[:, h * dk:h * dk + half] = ((x1 * cos - x2 * sin) * mult).astype(BF16)
            out_ref[:, h * dk + half:(h + 1) * dk] = ((x1 * sin + x2 * cos) * mult).astype(BF16)

    def store(out_ref, c0):
        def run(z):
            out_ref[:, c0:c0 + D] = z.astype(BF16)
        return run

    epilogues = [
        (0, lambda z: rope_store(z, q_ref, 1.0)),
        (D, lambda z: rope_store(z, k_ref, k_scale)),
        (2 * D, store(v_ref, 0)),
        (3 * D, store(v_ref, D)),
        (4 * D, store(gret_ref, 0)),
        (5 * D, store(gret_ref, D)),
        (8 * D, store(gr_ref, 0)),
        (9 * D, store(gc_ref, 0)),
    ]
    rb = tm // len(epilogues)
    anchor = anchor_of(glu_b)
    for j, (c0, epilogue) in enumerate(epilogues):
        z = proj(c0)
        _causal_conv_rows(win_ref, wdw_ref, bdw_ref, cv_ref, j * rb, rb, anchor)
        epilogue(z)
        anchor = anchor_of(z)

    win_ref[:, 0:CONV_HALO, :] = win_ref[:, tm:tm + CONV_HALO, :]


def _inproj(h, mod3, positions, g_norm, w_in, w_dw, b_dw, tm=512):
    B, S, D = h.shape
    dk = D // RET_HEADS
    half = dk // 2
    nt = S // tm
    freqs = np.power(ROPE_BASE, -np.arange(0, half, dtype=np.float64) * (2.0 / dk))
    step = np.arange(tm, dtype=np.float64)[:, None] * freqs[None, :]
    ctab, stab = jnp.asarray(np.cos(step), F32), jnp.asarray(np.sin(step), F32)
    contig = jnp.all(positions[:, 1:] - positions[:, :-1] == 1).astype(jnp.int32).reshape(1, 1)
    row = lambda n: pl.BlockSpec((None, tm, n), lambda b, i: (b, i, 0))
    out = lambda n: jax.ShapeDtypeStruct((B, S, n), BF16)
    kern = functools.partial(_inproj_kernel, k_scale=float(dk) ** -0.5)
    return pl.pallas_call(
        kern,
        grid=(B, nt),
        in_specs=[pl.BlockSpec(memory_space=pltpu.SMEM),
                  row(D),
                  pl.BlockSpec((None, 1, mod3.shape[-1]), lambda b, i: (b, 0, 0)),
                  row(1),
                  _resident((1, half)), _resident((tm, half)), _resident((tm, half)),
                  _resident((1, D)),
                  _resident(w_in.shape),
                  _resident((CONV_WIDTH, D)), _resident((1, D))],
        out_specs=[row(D), row(D), row(2 * D), row(2 * D), row(D), row(D), row(D)],
        out_shape=[out(D), out(D), out(2 * D), out(2 * D), out(D), out(D), out(D)],
        scratch_shapes=[pltpu.VMEM((D // LANES, CONV_HALO + tm, LANES), F32), pltpu.VMEM((tm, half), F32),
                        pltpu.VMEM((tm, half), F32)],
        compiler_params=_params("parallel", "arbitrary"),
        name="inproj",
    )(contig, h, mod3, positions.reshape(B, S, 1), jnp.asarray(freqs, F32).reshape(1, half), ctab, stab,
      g_norm.reshape(1, D), w_in, w_dw, b_dw.reshape(1, D))


def _retention_kernel(q_ref, k_ref, v_ref, intra_ref, xi_ref, zeta_ref, o_ref, state_ref, *, gamma_c):
    dk = q_ref.shape[-1] // RET_HEADS
    dv = v_ref.shape[-1] // RET_HEADS

    @pl.when(pl.program_id(1) == 0)
    def _():
        state_ref[...] = jnp.zeros(state_ref.shape, F32)

    chunk = intra_ref.shape[-1]
    for c in range(q_ref.shape[0] // chunk):
        rows = slice(c * chunk, (c + 1) * chunk)
        for h in range(RET_HEADS):
            qh = q_ref[rows, h * dk:(h + 1) * dk]
            kh = k_ref[rows, h * dk:(h + 1) * dk]
            vh = v_ref[rows, h * dv:(h + 1) * dv]
            s = lax.dot_general(qh, kh, (((1,), (1,)), ((), ())), preferred_element_type=F32) * intra_ref[h]
            st = state_ref[h]
            o = _dot(s.astype(BF16), vh) + _dot(qh, st.astype(BF16)) * xi_ref[h]
            o_ref[rows, h * dv:(h + 1) * dv] = o.astype(o_ref.dtype)
            kz = (kh.astype(F32) * zeta_ref[h]).astype(BF16)
            state_ref[h] = st * gamma_c[h] + lax.dot_general(
                kz, vh, (((0,), (0,)), ((), ())), preferred_element_type=F32)


def _retention_tables(chunk):
    log_gamma = np.log1p(-np.power(2.0, -5.0 - np.arange(RET_HEADS, dtype=np.float64)))
    idx = np.arange(chunk, dtype=np.float64)
    diff = idx[:, None] - idx[None, :]
    intra = np.where(diff[None] >= 0, np.exp(log_gamma[:, None, None] * np.maximum(diff, 0.0)[None]), 0.0)
    xi = np.exp(log_gamma[:, None] * (idx + 1.0))[:, :, None]
    zeta = np.exp(log_gamma[:, None] * (chunk - 1.0 - idx))[:, :, None]
    gamma_c = tuple(float(g) for g in np.exp(log_gamma * chunk))
    return (jnp.asarray(intra, F32), jnp.asarray(xi, F32), jnp.asarray(zeta, F32), gamma_c)


def _retention(q, k, v, chunk=RET_CHUNK, chunks_per_step=2):
    B, S, QK = q.shape
    V = v.shape[-1]
    dk, dv = QK // RET_HEADS, V // RET_HEADS
    intra, xi, zeta, gamma_c = _retention_tables(chunk)
    rows = chunk * chunks_per_step
    row = lambda n: pl.BlockSpec((None, rows, n), lambda b, i: (b, i, 0))
    kern = functools.partial(_retention_kernel, gamma_c=gamma_c)
    return pl.pallas_call(
        kern,
        grid=(B, S // rows),
        in_specs=[row(QK), row(QK), row(V),
                  _resident(intra.shape), _resident(xi.shape), _resident(zeta.shape)],
        out_specs=row(V),
        out_shape=jax.ShapeDtypeStruct((B, S, V), BF16),
        scratch_shapes=[pltpu.VMEM((RET_HEADS, dk, dv), F32)],
        compiler_params=_params("parallel", "arbitrary"),
        name="retention",
    )(q, k, v, intra, xi, zeta)


def _merge_pieces(h1_ref, mod_ref, o_ref_in, gret_ref, gr_ref, gc_ref, cv_ref, gng_ref, gnb_ref,
                  lng_ref, lnb_ref, wro_ref, wco_ref, bco_ref, wout_ref, h2_out):
    D = h1_ref.shape[-1]
    dv = o_ref_in.shape[-1] // RET_HEADS
    state = {}

    def conv_tail(zero):
        cv = cv_ref[...].astype(F32)
        mu = jnp.mean(cv, axis=-1, keepdims=True)
        d = cv - mu
        var = jnp.mean(d * d, axis=-1, keepdims=True)
        ln = d * lax.rsqrt(var + EPS) * lng_ref[...] + (lnb_ref[...] + zero)
        state["y_conv"] = _dot(_silu(ln).astype(BF16), wco_ref[...]) + bco_ref[...]

    def head_tail(h):
        def run(zero):
            cols = slice(h * dv, (h + 1) * dv)
            o = o_ref_in[:, cols].astype(F32)
            mu = jnp.mean(o, axis=-1, keepdims=True)
            d = o - mu
            var = jnp.mean(d * d, axis=-1, keepdims=True)
            y = d * lax.rsqrt(var + EPS) * gng_ref[:, cols] + (gnb_ref[:, cols] + zero)
            og = (_silu(gret_ref[:, cols].astype(F32)) * y).astype(BF16)
            part = _dot(og, wro_ref[cols, :])
            state["y_ret"] = part if h == 0 else state["y_ret"] + part
        return run

    def merge(zero):
        _, _, gate = _mod_slices(mod_ref, 0, D)
        merged = (_sigmoid(gr_ref[...].astype(F32) + zero) * state["y_ret"]
                  + _sigmoid(gc_ref[...].astype(F32) + zero) * state["y_conv"])
        h2_out(h1_ref[...] + gate * _dot(merged.astype(BF16), wout_ref[...]))

    return [conv_tail] + [head_tail(h) for h in range(RET_HEADS)] + [merge]


def _tail_kernel(h1_ref, modm_ref, o_ref_in, gret_ref, gr_ref, gc_ref, cv_ref, modf_ref,
                 gng_ref, gnb_ref, lng_ref, lnb_ref, wro_ref, wco_ref, bco_ref, wout_ref,
                 g3_ref, wgu_ref, wd_ref, gf_ref, out_ref, h2_ref, *, ck, final_norm):
    t = pl.program_id(0)

    @pl.when(t == 0)
    def _():
        h2_ref[...] = jnp.zeros(h2_ref.shape, F32)

    def h2_out(h2):
        h2_ref[t % 2] = h2

    side = _merge_pieces(h1_ref, modm_ref, o_ref_in, gret_ref, gr_ref, gc_ref, cv_ref, gng_ref, gnb_ref,
                         lng_ref, lnb_ref, wro_ref, wco_ref, bco_ref, wout_ref, h2_out)
    out_ref[...] = _ffn_math(h2_ref[(t + 1) % 2], modf_ref, g3_ref, wgu_ref, wd_ref, gf_ref,
                             sub=0, ck=ck, final_norm=final_norm, side=side)


def _tail(h1, mod3, o, gret, gr, gc, cv, gn_g, gn_b, ln_g, ln_b, w_ret_o, w_conv_o, b_conv_o, w_out,
          g_norm3, w_gu, w_d, g_final, *, final_norm, tm=512, ck=256):
    B, S, D = h1.shape
    V = o.shape[-1]
    Fh = w_d.shape[0]
    ntb = S // tm
    last_tile = B * ntb - 1

    def merged_tile(t):
        m = jnp.minimum(t, last_tile)
        return m // ntb, m % ntb

    def ffn_tile(t):
        f = jnp.maximum(t - 1, 0)
        return f // ntb, f % ntb

    row = lambda n: pl.BlockSpec((None, tm, n), lambda t: (*merged_tile(t), 0))
    mod_merge, mod_ffn = mod3[:, :, 3 * D:6 * D], mod3[:, :, 6 * D:9 * D]
    mod_block = lambda tile: pl.BlockSpec((None, 1, 3 * D), lambda t: (tile(t)[0], 0, 0))
    kern = functools.partial(_tail_kernel, ck=ck, final_norm=final_norm)
    return pl.pallas_call(
        kern,
        grid=(last_tile + 2,),
        in_specs=[row(D), mod_block(merged_tile), row(V), row(V), row(D), row(D), row(D), mod_block(ffn_tile),
                  _resident((1, V)), _resident((1, V)), _resident((1, D)), _resident((1, D)),
                  _resident((V, D)), _resident((D, D)), _resident((1, D)), _resident((D, D)),
                  _resident((1, D)), _resident((D, 2 * Fh)), _resident((Fh, D)), _resident((1, D))],
        out_specs=pl.BlockSpec((None, tm, D), lambda t: (*ffn_tile(t), 0)),
        out_shape=jax.ShapeDtypeStruct((B, S, D), F32),
        scratch_shapes=[pltpu.VMEM((2, tm, D), F32)],
        compiler_params=_params("arbitrary", vmem_limit_bytes=TAIL_VMEM_LIMIT_BYTES),
        name="tail",
    )(h1, mod_merge, o, gret, gr, gc, cv, mod_ffn, gn_g.reshape(1, V), gn_b.reshape(1, V), ln_g.reshape(1, D),
      ln_b.reshape(1, D), w_ret_o, w_conv_o, b_conv_o.reshape(1, D), w_out,
      g_norm3.reshape(1, D), w_gu, w_d, g_final.reshape(1, D))


def kernel(x, c, positions, w_mod, b_mod, g_norm1, w_ffn1_gu, w_ffn1_d, g_norm2, w_in, ret_gn_g, ret_gn_b,
           w_ret_o, w_dw, b_dw, conv_ln_g, conv_ln_b, w_conv_o, b_conv_o, w_out, g_norm3, w_ffn2_gu,
           w_ffn2_d, g_normf):
    B, S, D = x.shape
    depth = w_mod.shape[0]
    bf = lambda w: w.astype(BF16)

    h = x
    for l in range(depth):
        last = l == depth - 1
        mod3 = _mod(c, w_mod[l], b_mod[l]).reshape(B, 1, 3 * N_SUB * D)
        h = _ffn(h, mod3, g_norm1[l], bf(w_ffn1_gu[l]), bf(w_ffn1_d[l]), g_normf, sub=0, final_norm=False)
        q, k, v, gret, gr, gc, cv = _inproj(h, mod3, positions, g_norm2[l], bf(w_in[l]),
                                            w_dw[l].reshape(CONV_WIDTH, D), b_dw[l])
        o = _retention(q, k, v)
        h = _tail(h, mod3, o, gret, gr, gc, cv, ret_gn_g[l], ret_gn_b[l], conv_ln_g[l], conv_ln_b[l],
                  bf(w_ret_o[l]), bf(w_conv_o[l]), b_conv_o[l], bf(w_out[l]),
                  g_norm3[l], bf(w_ffn2_gu[l]), bf(w_ffn2_d[l]), g_normf, final_norm=last)
    return h
```

```python
import functools

import numpy as np
import jax
import jax.numpy as jnp
from jax import lax
from jax.experimental import pallas as pl
from jax.experimental.pallas import tpu as pltpu

F32 = jnp.float32
BF16 = jnp.bfloat16

EPS = 1e-6
N_SUB = 3
RET_HEADS = 4
CONV_WIDTH = 31
ROPE_BASE = 10000.0
RET_CHUNK = 256
CONV_HALO = 32
LANES = 128
SUBLANES = 8
VMEM_LIMIT_BYTES = 56 * 1024 * 1024
TAIL_VMEM_LIMIT_BYTES = 62 * 1024 * 1024


def _sigmoid(x):
    return 0.5 * jnp.tanh(0.5 * x) + 0.5


def _silu(x):
    h = 0.5 * x
    return h + h * jnp.tanh(h)


def _rms_norm(x, g):
    return x * lax.rsqrt(jnp.mean(x * x, axis=-1, keepdims=True) + EPS) * g


def _dot(a, b):
    return jnp.dot(a, b, preferred_element_type=F32)


def _resident(shape):
    zeros = (0,) * len(shape)
    return pl.BlockSpec(shape, lambda *_: zeros, pipeline_mode=pl.Buffered(1))


def _params(*semantics, vmem_limit_bytes=VMEM_LIMIT_BYTES):
    return pltpu.CompilerParams(dimension_semantics=semantics, vmem_limit_bytes=vmem_limit_bytes)


def _mod_kernel(c_ref, w_ref, b_ref, o_ref):
    c = c_ref[...]
    o_ref[...] = _dot(_silu(c).astype(BF16), w_ref[...].astype(BF16)) + b_ref[...]


def _mod(c, w_mod, b_mod, tn=1024):
    B, D = c.shape
    N = w_mod.shape[1]
    return pl.pallas_call(
        _mod_kernel,
        grid=(N // tn,),
        in_specs=[_resident((B, D)),
                  pl.BlockSpec((D, tn), lambda j: (0, j)),
                  pl.BlockSpec((1, tn), lambda j: (0, j))],
        out_specs=pl.BlockSpec((B, tn), lambda j: (0, j)),
        out_shape=jax.ShapeDtypeStruct((B, N), F32),
        compiler_params=_params("parallel"),
        name="mod",
    )(c, w_mod, b_mod.reshape(1, N))


def _mod_slices(mod_ref, sub, D):
    base = 3 * sub * D
    return (mod_ref[:, base:base + D], mod_ref[:, base + D:base + 2 * D],
            mod_ref[:, base + 2 * D:base + 3 * D])


def _bf16(w):
    return w if w.dtype == BF16 else w.astype(BF16)


def _zero_of(v):
    return v[0:1, 0:1] * 0.0


def _ffn_math(x, mod_ref, g_ref, wgu_ref, wd_ref, gf_ref, *, sub, ck, final_norm, side=()):
    D = x.shape[-1]
    Fh = wd_ref.shape[0]
    shift, scale, gate = _mod_slices(mod_ref, sub, D)
    xb = (_rms_norm(x, g_ref[...]) * (1.0 + scale) + shift).astype(BF16)
    acc = None
    zeros = []
    for j in range(Fh // ck):
        g = _dot(xb, _bf16(wgu_ref[:, j * ck:(j + 1) * ck]))
        u = _dot(xb, _bf16(wgu_ref[:, Fh + j * ck:Fh + (j + 1) * ck]))
        zeros.append(_zero_of(g))
        part = _dot((_silu(g) * u).astype(BF16), _bf16(wd_ref[j * ck:(j + 1) * ck, :]))
        acc = part if acc is None else acc + part
    h = x + (0.5 * gate) * acc
    if final_norm:
        h = _rms_norm(h, gf_ref[...])
    for piece, zero in zip(side, zeros):
        piece(zero)
    return h


def _ffn_kernel(x_ref, mod_ref, g_ref, wgu_ref, wd_ref, gf_ref, o_ref, *, sub, ck, final_norm):
    o_ref[...] = _ffn_math(x_ref[...], mod_ref, g_ref, wgu_ref, wd_ref, gf_ref,
                           sub=sub, ck=ck, final_norm=final_norm)


def _ffn(x, mod3, g_norm, w_gu, w_d, g_final, *, sub, final_norm, tm=512, ck=256):
    B, S, D = x.shape
    Fh = w_d.shape[0]
    row = pl.BlockSpec((None, tm, D), lambda b, i: (b, i, 0))
    kern = functools.partial(_ffn_kernel, sub=sub, ck=ck, final_norm=final_norm)
    return pl.pallas_call(
        kern,
        grid=(B, S // tm),
        in_specs=[row,
                  pl.BlockSpec((None, 1, mod3.shape[-1]), lambda b, i: (b, 0, 0)),
                  _resident((1, D)),
                  _resident((D, 2 * Fh)),
                  _resident((Fh, D)),
                  _resident((1, D))],
        out_specs=row,
        out_shape=jax.ShapeDtypeStruct((B, S, D), F32),
        compiler_params=_params("parallel", "parallel"),
        name=f"ffn{sub}",
    )(x, mod3, g_norm.reshape(1, D), w_gu, w_d, g_final.reshape(1, D))


def _causal_conv_rows(win_ref, wdw_ref, bdw_ref, cv_ref, r0, rb, anchor):
    D = cv_ref.shape[-1]
    off0 = CONV_HALO - (CONV_WIDTH - 1)
    for c0 in range(0, D, LANES):
        wv = win_ref[c0 // LANES, r0:r0 + rb + CONV_HALO, :]
        acc = jnp.broadcast_to(bdw_ref[:, c0:c0 + LANES] + anchor, (rb, LANES))
        for m in range(SUBLANES):
            rows = rb if m == 0 else rb + SUBLANES
            g = None
            for p in range(m, off0 + CONV_WIDTH, SUBLANES):
                if p < off0:
                    continue
                term = wv[p - m:p - m + rows] * wdw_ref[p - off0:p - off0 + 1, c0:c0 + LANES]
                g = term if g is None else g + term
            acc = acc + g[m:m + rb]
        cv_ref[r0:r0 + rb, c0:c0 + LANES] = acc.astype(cv_ref.dtype)


def _rope_tables(contig_ref, pos_ref, freq_ref, ctab_ref, stab_ref, cos_ref, sin_ref):
    @pl.when(contig_ref[0, 0] != 0)
    def _():
        a0 = pos_ref[0:1, :].astype(F32) * freq_ref[...]
        c0, s0 = jnp.cos(a0), jnp.sin(a0)
        ct, st = ctab_ref[...], stab_ref[...]
        cos_ref[...] = c0 * ct - s0 * st
        sin_ref[...] = s0 * ct + c0 * st

    @pl.when(contig_ref[0, 0] == 0)
    def _():
        ang = pos_ref[...].astype(F32) * freq_ref[...]
        cos_ref[...] = jnp.cos(ang)
        sin_ref[...] = jnp.sin(ang)


def _inproj_kernel(contig_ref, x_ref, mod_ref, pos_ref, freq_ref, ctab_ref, stab_ref, g_ref, w_ref, wdw_ref,
                   bdw_ref, q_ref, k_ref, v_ref, gret_ref, gr_ref, gc_ref, cv_ref,
                   win_ref, cos_ref, sin_ref, *, k_scale):
    tm, D = x_ref.shape
    dk = D // RET_HEADS
    half = dk // 2

    @pl.when(pl.program_id(1) == 0)
    def _():
        win_ref[:, 0:CONV_HALO, :] = jnp.zeros((D // LANES, CONV_HALO, LANES), F32)

    _rope_tables(contig_ref, pos_ref, freq_ref, ctab_ref, stab_ref, cos_ref, sin_ref)

    x = x_ref[...]
    shift, scale, _ = _mod_slices(mod_ref, 1, D)
    xb = (_rms_norm(x, g_ref[...]) * (1.0 + scale) + shift).astype(BF16)

    def proj(c0):
        return _dot(xb, w_ref[:, c0:c0 + D])

    def anchor_of(z):
        return z[0:1, 0:LANES] * 0.0

    glu_a = proj(6 * D)
    glu_b = proj(7 * D)
    u = glu_a * _sigmoid(glu_b)
    for cb in range(D // LANES):
        win_ref[cb, CONV_HALO:CONV_HALO + tm, :] = u[:, cb * LANES:(cb + 1) * LANES]

    def rope_store(z, out_ref, mult):
        cos = cos_ref[...]
        sin = sin_ref[...]
        for h in range(RET_HEADS):
            x1 = z[:, h * dk:h * dk + half]
            x2 = z[:, h * dk + half:(h + 1) * dk]
            out_ref[:, h * dk:h * dk + half] = ((x1 * cos - x2 * sin) * mult).astype(BF16)
            out_ref[:, h * dk + half:(h + 1) * dk] = ((x1 * sin + x2 * cos) * mult).astype(BF16)

    def store(out_ref, c0):
        def run(z):
            out_ref[:, c0:c0 + D] = z.astype(BF16)
        return run

    epilogues = [
        (0, lambda z: rope_store(z, q_ref, 1.0)),
        (D, lambda z: rope_store(z, k_ref, k_scale)),
        (2 * D, store(v_ref, 0)),
        (3 * D, store(v_ref, D)),
        (4 * D, store(gret_ref, 0)),
        (5 * D, store(gret_ref, D)),
        (8 * D, store(gr_ref, 0)),
        (9 * D, store(gc_ref, 0)),
    ]
    rb = tm // len(epilogues)
    anchor = anchor_of(glu_b)
    for j, (c0, epilogue) in enumerate(epilogues):
        z = proj(c0)
        _causal_conv_rows(win_ref, wdw_ref, bdw_ref, cv_ref, j * rb, rb, anchor)
        epilogue(z)
        anchor = anchor_of(z)

    win_ref[:, 0:CONV_HALO, :] = win_ref[:, tm:tm + CONV_HALO, :]


def _inproj(h, mod3, positions, g_norm, w_in, w_dw, b_dw, tm=512):
    B, S, D = h.shape
    dk = D // RET_HEADS
    half = dk // 2
    nt = S // tm
    freqs = np.power(ROPE_BASE, -np.arange(0, half, dtype=np.float64) * (2.0 / dk))
    step = np.arange(tm, dtype=np.float64)[:, None] * freqs[None, :]
    ctab, stab = jnp.asarray(np.cos(step), F32), jnp.asarray(np.sin(step), F32)
    contig = jnp.all(positions[:, 1:] - positions[:, :-1] == 1).astype(jnp.int32).reshape(1, 1)
    row = lambda n: pl.BlockSpec((None, tm, n), lambda b, i: (b, i, 0))
    out = lambda n: jax.ShapeDtypeStruct((B, S, n), BF16)
    kern = functools.partial(_inproj_kernel, k_scale=float(dk) ** -0.5)
    return pl.pallas_call(
        kern,
        grid=(B, nt),
        in_specs=[pl.BlockSpec(memory_space=pltpu.SMEM),
                  row(D),
                  pl.BlockSpec((None, 1, mod3.shape[-1]), lambda b, i: (b, 0, 0)),
                  row(1),
                  _resident((1, half)), _resident((tm, half)), _resident((tm, half)),
                  _resident((1, D)),
                  _resident(w_in.shape),
                  _resident((CONV_WIDTH, D)), _resident((1, D))],
        out_specs=[row(D), row(D), row(2 * D), row(2 * D), row(D), row(D), row(D)],
        out_shape=[out(D), out(D), out(2 * D), out(2 * D), out(D), out(D), out(D)],
        scratch_shapes=[pltpu.VMEM((D // LANES, CONV_HALO + tm, LANES), F32), pltpu.VMEM((tm, half), F32),
                        pltpu.VMEM((tm, half), F32)],
        compiler_params=_params("parallel", "arbitrary"),
        name="inproj",
    )(contig, h, mod3, positions.reshape(B, S, 1), jnp.asarray(freqs, F32).reshape(1, half), ctab, stab,
      g_norm.reshape(1, D), w_in, w_dw, b_dw.reshape(1, D))


def _retention_kernel(q_ref, k_ref, v_ref, intra_ref, xi_ref, zeta_ref, o_ref, state_ref, *, gamma_c):
    dk = q_ref.shape[-1] // RET_HEADS
    dv = v_ref.shape[-1] // RET_HEADS

    @pl.when(pl.program_id(1) == 0)
    def _():
        state_ref[...] = jnp.zeros(state_ref.shape, F32)

    chunk = intra_ref.shape[-1]
    for c in range(q_ref.shape[0] // chunk):
        rows = slice(c * chunk, (c + 1) * chunk)
        for h in range(RET_HEADS):
            qh = q_ref[rows, h * dk:(h + 1) * dk]
            kh = k_ref[rows, h * dk:(h + 1) * dk]
            vh = v_ref[rows, h * dv:(h + 1) * dv]
            s = lax.dot_general(qh, kh, (((1,), (1,)), ((), ())), preferred_element_type=F32) * intra_ref[h]
            st = state_ref[h]
            o = _dot(s.astype(BF16), vh) + _dot(qh, st.astype(BF16)) * xi_ref[h]
            o_ref[rows, h * dv:(h + 1) * dv] = o.astype(o_ref.dtype)
            kz = (kh.astype(F32) * zeta_ref[h]).astype(BF16)
            state_ref[h] = st * gamma_c[h] + lax.dot_general(
                kz, vh, (((0,), (0,)), ((), ())), preferred_element_type=F32)


def _retention_tables(chunk):
    log_gamma = np.log1p(-np.power(2.0, -5.0 - np.arange(RET_HEADS, dtype=np.float64)))
    idx = np.arange(chunk, dtype=np.float64)
    diff = idx[:, None] - idx[None, :]
    intra = np.where(diff[None] >= 0, np.exp(log_gamma[:, None, None] * np.maximum(diff, 0.0)[None]), 0.0)
    xi = np.exp(log_gamma[:, None] * (idx + 1.0))[:, :, None]
    zeta = np.exp(log_gamma[:, None] * (chunk - 1.0 - idx))[:, :, None]
    gamma_c = tuple(float(g) for g in np.exp(log_gamma * chunk))
    return (jnp.asarray(intra, F32), jnp.asarray(xi, F32), jnp.asarray(zeta, F32), gamma_c)


def _retention(q, k, v, chunk=RET_CHUNK, chunks_per_step=4):
    B, S, QK = q.shape
    V = v.shape[-1]
    dk, dv = QK // RET_HEADS, V // RET_HEADS
    intra, xi, zeta, gamma_c = _retention_tables(chunk)
    rows = chunk * chunks_per_step
    row = lambda n: pl.BlockSpec((None, rows, n), lambda b, i: (b, i, 0))
    kern = functools.partial(_retention_kernel, gamma_c=gamma_c)
    return pl.pallas_call(
        kern,
        grid=(B, S // rows),
        in_specs=[row(QK), row(QK), row(V),
                  _resident(intra.shape), _resident(xi.shape), _resident(zeta.shape)],
        out_specs=row(V),
        out_shape=jax.ShapeDtypeStruct((B, S, V), BF16),
        scratch_shapes=[pltpu.VMEM((RET_HEADS, dk, dv), F32)],
        compiler_params=_params("parallel", "arbitrary"),
        name="retention",
    )(q, k, v, intra, xi, zeta)


def _merge_pieces(h1_ref, mod_ref, o_ref_in, gret_ref, gr_ref, gc_ref, cv_ref, gng_ref, gnb_ref,
                  lng_ref, lnb_ref, wro_ref, wco_ref, bco_ref, wout_ref, h2_out):
    D = h1_ref.shape[-1]
    dv = o_ref_in.shape[-1] // RET_HEADS
    state = {}

    def conv_tail(zero):
        cv = cv_ref[...].astype(F32)
        mu = jnp.mean(cv, axis=-1, keepdims=True)
        d = cv - mu
        var = jnp.mean(d * d, axis=-1, keepdims=True)
        ln = d * lax.rsqrt(var + EPS) * lng_ref[...] + (lnb_ref[...] + zero)
        state["y_conv"] = _dot(_silu(ln).astype(BF16), wco_ref[...]) + bco_ref[...]

    def head_tail(h):
        def run(zero):
            cols = slice(h * dv, (h + 1) * dv)
            o = o_ref_in[:, cols].astype(F32)
            mu = jnp.mean(o, axis=-1, keepdims=True)
            d = o - mu
            var = jnp.mean(d * d, axis=-1, keepdims=True)
            y = d * lax.rsqrt(var + EPS) * gng_ref[:, cols] + (gnb_ref[:, cols] + zero)
            og = (_silu(gret_ref[:, cols].astype(F32)) * y).astype(BF16)
            part = _dot(og, wro_ref[cols, :])
            state["y_ret"] = part if h == 0 else state["y_ret"] + part
        return run

    def merge(zero):
        _, _, gate = _mod_slices(mod_ref, 0, D)
        merged = (_sigmoid(gr_ref[...].astype(F32) + zero) * state["y_ret"]
                  + _sigmoid(gc_ref[...].astype(F32) + zero) * state["y_conv"])
        h2_out(h1_ref[...] + gate * _dot(merged.astype(BF16), wout_ref[...]))

    return [conv_tail] + [head_tail(h) for h in range(RET_HEADS)] + [merge]


def _tail_kernel(h1_ref, modm_ref, o_ref_in, gret_ref, gr_ref, gc_ref, cv_ref, modf_ref,
                 gng_ref, gnb_ref, lng_ref, lnb_ref, wro_ref, wco_ref, bco_ref, wout_ref,
                 g3_ref, wgu_ref, wd_ref, gf_ref, out_ref, h2_ref, *, ck, final_norm):
    t = pl.program_id(0)

    @pl.when(t == 0)
    def _():
        h2_ref[...] = jnp.zeros(h2_ref.shape, F32)

    def h2_out(h2):
        h2_ref[t % 2] = h2

    side = _merge_pieces(h1_ref, modm_ref, o_ref_in, gret_ref, gr_ref, gc_ref, cv_ref, gng_ref, gnb_ref,
                         lng_ref, lnb_ref, wro_ref, wco_ref, bco_ref, wout_ref, h2_out)
    out_ref[...] = _ffn_math(h2_ref[(t + 1) % 2], modf_ref, g3_ref, wgu_ref, wd_ref, gf_ref,
                             sub=0, ck=ck, final_norm=final_norm, side=side)


def _tail(h1, mod3, o, gret, gr, gc, cv, gn_g, gn_b, ln_g, ln_b, w_ret_o, w_conv_o, b_conv_o, w_out,
          g_norm3, w_gu, w_d, g_final, *, final_norm, tm=512, ck=256):
    B, S, D = h1.shape
    V = o.shape[-1]
    Fh = w_d.shape[0]
    ntb = S // tm
    last_tile = B * ntb - 1

    def merged_tile(t):
        m = jnp.minimum(t, last_tile)
        return m // ntb, m % ntb

    def ffn_tile(t):
        f = jnp.maximum(t - 1, 0)
        return f // ntb, f % ntb

    row = lambda n: pl.BlockSpec((None, tm, n), lambda t: (*merged_tile(t), 0))
    mod_merge, mod_ffn = mod3[:, :, 3 * D:6 * D], mod3[:, :, 6 * D:9 * D]
    mod_block = lambda tile: pl.BlockSpec((None, 1, 3 * D), lambda t: (tile(t)[0], 0, 0))
    kern = functools.partial(_tail_kernel, ck=ck, final_norm=final_norm)
    return pl.pallas_call(
        kern,
        grid=(last_tile + 2,),
        in_specs=[row(D), mod_block(merged_tile), row(V), row(V), row(D), row(D), row(D), mod_block(ffn_tile),
                  _resident((1, V)), _resident((1, V)), _resident((1, D)), _resident((1, D)),
                  _resident((V, D)), _resident((D, D)), _resident((1, D)), _resident((D, D)),
                  _resident((1, D)), _resident((D, 2 * Fh)), _resident((Fh, D)), _resident((1, D))],
        out_specs=pl.BlockSpec((None, tm, D), lambda t: (*ffn_tile(t), 0)),
        out_shape=jax.ShapeDtypeStruct((B, S, D), F32),
        scratch_shapes=[pltpu.VMEM((2, tm, D), F32)],
        compiler_params=_params("arbitrary", vmem_limit_bytes=TAIL_VMEM_LIMIT_BYTES),
        name="tail",
    )(h1, mod_merge, o, gret, gr, gc, cv, mod_ffn, gn_g.reshape(1, V), gn_b.reshape(1, V), ln_g.reshape(1, D),
      ln_b.reshape(1, D), w_ret_o, w_conv_o, b_conv_o.reshape(1, D), w_out,
      g_norm3.reshape(1, D), w_gu, w_d, g_final.reshape(1, D))


def kernel(x, c, positions, w_mod, b_mod, g_norm1, w_ffn1_gu, w_ffn1_d, g_norm2, w_in, ret_gn_g, ret_gn_b,
           w_ret_o, w_dw, b_dw, conv_ln_g, conv_ln_b, w_conv_o, b_conv_o, w_out, g_norm3, w_ffn2_gu,
           w_ffn2_d, g_normf):
    B, S, D = x.shape
    depth = w_mod.shape[0]
    bf = lambda w: w.astype(BF16)

    h = x
    for l in range(depth):
        last = l == depth - 1
        mod3 = _mod(c, w_mod[l], b_mod[l]).reshape(B, 1, 3 * N_SUB * D)
        h = _ffn(h, mod3, g_norm1[l], w_ffn1_gu[l], w_ffn1_d[l], g_normf, sub=0, final_norm=False)
        q, k, v, gret, gr, gc, cv = _inproj(h, mod3, positions, g_norm2[l], bf(w_in[l]),
                                            w_dw[l].reshape(CONV_WIDTH, D), b_dw[l])
        o = _retention(q, k, v)
        h = _tail(h, mod3, o, gret, gr, gc, cv, ret_gn_g[l], ret_gn_b[l], conv_ln_g[l], conv_ln_b[l],
                  bf(w_ret_o[l]), bf(w_conv_o[l]), b_conv_o[l], bf(w_out[l]),
                  g_norm3[l], bf(w_ffn2_gu[l]), bf(w_ffn2_d[l]), g_normf, final_norm=last)
    return h
```

```python
import functools

import numpy as np
import jax
import jax.numpy as jnp
from jax import lax
from jax.experimental import pallas as pl
from jax.experimental.pallas import tpu as pltpu

F32 = jnp.float32
BF16 = jnp.bfloat16

EPS = 1e-6
N_SUB = 3
RET_HEADS = 4
CONV_WIDTH = 31
ROPE_BASE = 10000.0
RET_CHUNK = 256
CONV_HALO = 32
LANES = 128
SUBLANES = 8
VMEM_LIMIT_BYTES = 56 * 1024 * 1024
TAIL_VMEM_LIMIT_BYTES = 62 * 1024 * 1024


def _sigmoid(x):
    return 0.5 * jnp.tanh(0.5 * x) + 0.5


def _silu(x):
    h = 0.5 * x
    return h + h * jnp.tanh(h)


def _rms_norm(x, g):
    return x * lax.rsqrt(jnp.mean(x * x, axis=-1, keepdims=True) + EPS) * g


def _dot(a, b):
    return jnp.dot(a, b, preferred_element_type=F32)


def _resident(shape):
    zeros = (0,) * len(shape)
    return pl.BlockSpec(shape, lambda *_: zeros, pipeline_mode=pl.Buffered(1))


def _params(*semantics, vmem_limit_bytes=VMEM_LIMIT_BYTES):
    return pltpu.CompilerParams(dimension_semantics=semantics, vmem_limit_bytes=vmem_limit_bytes)


def _mod_kernel(c_ref, w_ref, b_ref, o_ref):
    c = c_ref[...]
    o_ref[...] = _dot(_silu(c).astype(BF16), w_ref[...].astype(BF16)) + b_ref[...]


def _mod(c, w_mod, b_mod, tn=1024):
    B, D = c.shape
    N = w_mod.shape[1]
    return pl.pallas_call(
        _mod_kernel,
        grid=(N // tn,),
        in_specs=[_resident((B, D)),
                  pl.BlockSpec((D, tn), lambda j: (0, j)),
                  pl.BlockSpec((1, tn), lambda j: (0, j))],
        out_specs=pl.BlockSpec((B, tn), lambda j: (0, j)),
        out_shape=jax.ShapeDtypeStruct((B, N), F32),
        compiler_params=_params("parallel"),
        name="mod",
    )(c, w_mod, b_mod.reshape(1, N))


def _mod_slices(mod_ref, sub, D):
    base = 3 * sub * D
    return (mod_ref[:, base:base + D], mod_ref[:, base + D:base + 2 * D],
            mod_ref[:, base + 2 * D:base + 3 * D])


def _bf16(w):
    return w if w.dtype == BF16 else w.astype(BF16)


def _zero_of(v):
    return v[0:1, 0:1] * 0.0


def _ffn_math(x, mod_ref, g_ref, wgu_ref, wd_ref, gf_ref, *, sub, ck, final_norm, side=()):
    D = x.shape[-1]
    Fh = wd_ref.shape[0]
    shift, scale, gate = _mod_slices(mod_ref, sub, D)
    xb = (_rms_norm(x, g_ref[...]) * (1.0 + scale) + shift).astype(BF16)
    acc = None
    zeros = []
    for c0 in range(0, Fh, ck):
        c1 = min(c0 + ck, Fh)
        g = _dot(xb, _bf16(wgu_ref[:, c0:c1]))
        u = _dot(xb, _bf16(wgu_ref[:, Fh + c0:Fh + c1]))
        zeros.append(_zero_of(g))
        part = _dot((_silu(g) * u).astype(BF16), _bf16(wd_ref[c0:c1, :]))
        acc = part if acc is None else acc + part
    h = x + (0.5 * gate) * acc
    if final_norm:
        h = _rms_norm(h, gf_ref[...])
    for piece, zero in zip(side, zeros):
        piece(zero)
    return h


def _ffn_kernel(x_ref, mod_ref, g_ref, wgu_ref, wd_ref, gf_ref, *rest, sub, ck, final_norm):
    n_cast = (len(rest) - 1) // 2
    o_ref = rest[n_cast]
    for src_ref, dst_ref in zip(rest[:n_cast], rest[n_cast + 1:]):
        dst_ref[...] = src_ref[...].astype(BF16)
    o_ref[...] = _ffn_math(x_ref[...], mod_ref, g_ref, wgu_ref, wd_ref, gf_ref,
                           sub=sub, ck=ck, final_norm=final_norm)


def _ffn(x, mod3, g_norm, w_gu, w_d, g_final, later_weights=(), *, sub, final_norm, tm=512, ck=256):
    B, S, D = x.shape
    Fh = w_d.shape[0]
    nt = S // tm
    steps = B * nt
    row = pl.BlockSpec((None, tm, D), lambda b, i: (b, i, 0))
    rides = [w.shape[0] % (steps * 2 * SUBLANES) == 0 for w in later_weights]
    riders = [w for w, r in zip(later_weights, rides) if r]
    cast_specs = [pl.BlockSpec((w.shape[0] // steps, w.shape[1]), lambda b, i: (b * nt + i, 0))
                  for w in riders]
    kern = functools.partial(_ffn_kernel, sub=sub, ck=ck, final_norm=final_norm)
    h, *cast = pl.pallas_call(
        kern,
        grid=(B, nt),
        in_specs=[row,
                  pl.BlockSpec((None, 1, mod3.shape[-1]), lambda b, i: (b, 0, 0)),
                  _resident((1, D)),
                  _resident((D, 2 * Fh)),
                  _resident((Fh, D)),
                  _resident((1, D))] + cast_specs,
        out_specs=[row] + cast_specs,
        out_shape=[jax.ShapeDtypeStruct((B, S, D), F32)]
                  + [jax.ShapeDtypeStruct(w.shape, BF16) for w in riders],
        compiler_params=_params("parallel", "parallel"),
        name=f"ffn{sub}",
    )(x, mod3, g_norm.reshape(1, D), w_gu, w_d, g_final.reshape(1, D), *riders)
    cast = iter(cast)
    return h, [next(cast) if r else w.astype(BF16) for w, r in zip(later_weights, rides)]


def _causal_conv_rows(win_ref, wdw_ref, bdw_ref, cv_ref, r0, rb, anchor):
    D = cv_ref.shape[-1]
    off0 = CONV_HALO - (CONV_WIDTH - 1)
    for c0 in range(0, D, LANES):
        wv = win_ref[c0 // LANES, r0:r0 + rb + CONV_HALO, :]
        acc = jnp.broadcast_to(bdw_ref[:, c0:c0 + LANES] + anchor, (rb, LANES))
        for m in range(SUBLANES):
            rows = rb if m == 0 else rb + SUBLANES
            g = None
            for p in range(m, off0 + CONV_WIDTH, SUBLANES):
                if p < off0:
                    continue
                term = wv[p - m:p - m + rows] * wdw_ref[p - off0:p - off0 + 1, c0:c0 + LANES]
                g = term if g is None else g + term
            acc = acc + g[m:m + rb]
        cv_ref[r0:r0 + rb, c0:c0 + LANES] = acc.astype(cv_ref.dtype)


def _rope_tables(contig_ref, pos_ref, freq_ref, ctab_ref, stab_ref, cos_ref, sin_ref):
    @pl.when(contig_ref[0, 0] != 0)
    def _():
        a0 = pos_ref[0:1, :].astype(F32) * freq_ref[...]
        c0, s0 = jnp.cos(a0), jnp.sin(a0)
        ct, st = ctab_ref[...], stab_ref[...]
        cos_ref[...] = c0 * ct - s0 * st
        sin_ref[...] = s0 * ct + c0 * st

    @pl.when(contig_ref[0, 0] == 0)
    def _():
        ang = pos_ref[...].astype(F32) * freq_ref[...]
        cos_ref[...] = jnp.cos(ang)
        sin_ref[...] = jnp.sin(ang)


def _inproj_kernel(contig_ref, x_ref, mod_ref, pos_ref, freq_ref, ctab_ref, stab_ref, g_ref, w_ref, wdw_ref,
                   bdw_ref, q_ref, k_ref, v_ref, gret_ref, gr_ref, gc_ref, cv_ref,
                   win_ref, cos_ref, sin_ref, *, k_scale):
    tm, D = x_ref.shape
    dk = D // RET_HEADS
    half = dk // 2

    @pl.when(pl.program_id(1) == 0)
    def _():
        win_ref[:, 0:CONV_HALO, :] = jnp.zeros((D // LANES, CONV_HALO, LANES), F32)

    _rope_tables(contig_ref, pos_ref, freq_ref, ctab_ref, stab_ref, cos_ref, sin_ref)

    x = x_ref[...]
    shift, scale, _ = _mod_slices(mod_ref, 1, D)
    xb = (_rms_norm(x, g_ref[...]) * (1.0 + scale) + shift).astype(BF16)

    def proj(c0):
        return _dot(xb, w_ref[:, c0:c0 + D])

    def anchor_of(z):
        return z[0:1, 0:LANES] * 0.0

    glu_a = proj(6 * D)
    glu_b = proj(7 * D)
    u = glu_a * _sigmoid(glu_b)
    for cb in range(D // LANES):
        win_ref[cb, CONV_HALO:CONV_HALO + tm, :] = u[:, cb * LANES:(cb + 1) * LANES]

    def rope_store(z, out_ref, mult):
        cos = cos_ref[...]
        sin = sin_ref[...]
        for h in range(RET_HEADS):
            x1 = z[:, h * dk:h * dk + half]
            x2 = z[:, h * dk + half:(h + 1) * dk]
            out_ref[:, h * dk:h * dk + half] = ((x1 * cos - x2 * sin) * mult).astype(BF16)
            out_ref[:, h * dk + half:(h + 1) * dk] = ((x1 * sin + x2 * cos) * mult).astype(BF16)

    def store(out_ref, c0):
        def run(z):
            out_ref[:, c0:c0 + D] = z.astype(BF16)
        return run

    epilogues = [
        (2 * D, store(v_ref, 0)),
        (3 * D, store(v_ref, D)),
        (4 * D, store(gret_ref, 0)),
        (5 * D, store(gret_ref, D)),
        (8 * D, store(gr_ref, 0)),
        (9 * D, store(gc_ref, 0)),
        (0, lambda z: rope_store(z, q_ref, 1.0)),
        (D, lambda z: rope_store(z, k_ref, k_scale)),
    ]
    n_conv = len(epilogues) - 2
    tiles = tm // (2 * SUBLANES)
    sizes = [2 * SUBLANES * (tiles // n_conv + (1 if j < tiles % n_conv else 0)) for j in range(n_conv)]
    r0 = 0
    for j, (c0, epilogue) in enumerate(epilogues):
        z = proj(c0)
        if j < n_conv:
            _causal_conv_rows(win_ref, wdw_ref, bdw_ref, cv_ref, r0, sizes[j], anchor_of(z))
            r0 += sizes[j]
        epilogue(z)

    win_ref[:, 0:CONV_HALO, :] = win_ref[:, tm:tm + CONV_HALO, :]


def _inproj(h, mod3, positions, g_norm, w_in, w_dw, b_dw, tm=512):
    B, S, D = h.shape
    dk = D // RET_HEADS
    half = dk // 2
    nt = S // tm
    freqs = np.power(ROPE_BASE, -np.arange(0, half, dtype=np.float64) * (2.0 / dk))
    step = np.arange(tm, dtype=np.float64)[:, None] * freqs[None, :]
    ctab, stab = jnp.asarray(np.cos(step), F32), jnp.asarray(np.sin(step), F32)
    contig = jnp.all(positions[:, 1:] - positions[:, :-1] == 1).astype(jnp.int32).reshape(1, 1)
    row = lambda n: pl.BlockSpec((None, tm, n), lambda b, i: (b, i, 0))
    out = lambda n: jax.ShapeDtypeStruct((B, S, n), BF16)
    kern = functools.partial(_inproj_kernel, k_scale=float(dk) ** -0.5)
    return pl.pallas_call(
        kern,
        grid=(B, nt),
        in_specs=[pl.BlockSpec(memory_space=pltpu.SMEM),
                  row(D),
                  pl.BlockSpec((None, 1, mod3.shape[-1]), lambda b, i: (b, 0, 0)),
                  row(1),
                  _resident((1, half)), _resident((tm, half)), _resident((tm, half)),
                  _resident((1, D)),
                  _resident(w_in.shape),
                  _resident((CONV_WIDTH, D)), _resident((1, D))],
        out_specs=[row(D), row(D), row(2 * D), row(2 * D), row(D), row(D), row(D)],
        out_shape=[out(D), out(D), out(2 * D), out(2 * D), out(D), out(D), out(D)],
        scratch_shapes=[pltpu.VMEM((D // LANES, CONV_HALO + tm, LANES), F32), pltpu.VMEM((tm, half), F32),
                        pltpu.VMEM((tm, half), F32)],
        compiler_params=_params("parallel", "arbitrary"),
        name="inproj",
    )(contig, h, mod3, positions.reshape(B, S, 1), jnp.asarray(freqs, F32).reshape(1, half), ctab, stab,
      g_norm.reshape(1, D), w_in, w_dw, b_dw.reshape(1, D))


def _retention_kernel(q_ref, k_ref, v_ref, intra_ref, xi_ref, zeta_ref, o_ref, state_ref, *, gamma_c):
    dk = q_ref.shape[-1] // RET_HEADS
    dv = v_ref.shape[-1] // RET_HEADS

    @pl.when(pl.program_id(1) == 0)
    def _():
        state_ref[...] = jnp.zeros(state_ref.shape, F32)

    chunk = intra_ref.shape[-1]
    for c in range(q_ref.shape[0] // chunk):
        rows = slice(c * chunk, (c + 1) * chunk)
        for h in range(RET_HEADS):
            qh = q_ref[rows, h * dk:(h + 1) * dk]
            kh = k_ref[rows, h * dk:(h + 1) * dk]
            vh = v_ref[rows, h * dv:(h + 1) * dv]
            s = lax.dot_general(qh, kh, (((1,), (1,)), ((), ())), preferred_element_type=F32) * intra_ref[h]
            st = state_ref[h]
            o = _dot(s.astype(BF16), vh) + _dot(qh, st.astype(BF16)) * xi_ref[h]
            o_ref[rows, h * dv:(h + 1) * dv] = o.astype(o_ref.dtype)
            kz = (kh.astype(F32) * zeta_ref[h]).astype(BF16)
            state_ref[h] = st * gamma_c[h] + lax.dot_general(
                kz, vh, (((0,), (0,)), ((), ())), preferred_element_type=F32)


def _retention_tables(chunk):
    log_gamma = np.log1p(-np.power(2.0, -5.0 - np.arange(RET_HEADS, dtype=np.float64)))
    idx = np.arange(chunk, dtype=np.float64)
    diff = idx[:, None] - idx[None, :]
    intra = np.where(diff[None] >= 0, np.exp(log_gamma[:, None, None] * np.maximum(diff, 0.0)[None]), 0.0)
    xi = np.exp(log_gamma[:, None] * (idx + 1.0))[:, :, None]
    zeta = np.exp(log_gamma[:, None] * (chunk - 1.0 - idx))[:, :, None]
    gamma_c = tuple(float(g) for g in np.exp(log_gamma * chunk))
    return (jnp.asarray(intra, F32), jnp.asarray(xi, F32), jnp.asarray(zeta, F32), gamma_c)


def _retention(q, k, v, chunk=RET_CHUNK, chunks_per_step=4):
    B, S, QK = q.shape
    V = v.shape[-1]
    dk, dv = QK // RET_HEADS, V // RET_HEADS
    intra, xi, zeta, gamma_c = _retention_tables(chunk)
    rows = chunk * chunks_per_step
    row = lambda n: pl.BlockSpec((None, rows, n), lambda b, i: (b, i, 0))
    kern = functools.partial(_retention_kernel, gamma_c=gamma_c)
    return pl.pallas_call(
        kern,
        grid=(B, S // rows),
        in_specs=[row(QK), row(QK), row(V),
                  _resident(intra.shape), _resident(xi.shape), _resident(zeta.shape)],
        out_specs=row(V),
        out_shape=jax.ShapeDtypeStruct((B, S, V), BF16),
        scratch_shapes=[pltpu.VMEM((RET_HEADS, dk, dv), F32)],
        compiler_params=_params("parallel", "arbitrary"),
        name="retention",
    )(q, k, v, intra, xi, zeta)


def _merge_pieces(h1_ref, mod_ref, o_ref_in, gret_ref, gr_ref, gc_ref, cv_ref, gng_ref, gnb_ref,
                  lng_ref, lnb_ref, wro_ref, wco_ref, bco_ref, wout_ref, h2_out):
    D = h1_ref.shape[-1]
    dv = o_ref_in.shape[-1] // RET_HEADS
    state = {}

    def conv_tail(zero):
        cv = cv_ref[...].astype(F32)
        mu = jnp.mean(cv, axis=-1, keepdims=True)
        d = cv - mu
        var = jnp.mean(d * d, axis=-1, keepdims=True)
        ln = d * lax.rsqrt(var + EPS) * lng_ref[...] + (lnb_ref[...] + zero)
        state["y_conv"] = _dot(_silu(ln).astype(BF16), wco_ref[...]) + bco_ref[...]

    def head_tail(h):
        def run(zero):
            cols = slice(h * dv, (h + 1) * dv)
            o = o_ref_in[:, cols].astype(F32)
            mu = jnp.mean(o, axis=-1, keepdims=True)
            d = o - mu
            var = jnp.mean(d * d, axis=-1, keepdims=True)
            y = d * lax.rsqrt(var + EPS) * gng_ref[:, cols] + (gnb_ref[:, cols] + zero)
            og = (_silu(gret_ref[:, cols].astype(F32)) * y).astype(BF16)
            part = _dot(og, wro_ref[cols, :])
            state["y_ret"] = part if h == 0 else state["y_ret"] + part
        return run

    def merge(zero):
        _, _, gate = _mod_slices(mod_ref, 0, D)
        merged = (_sigmoid(gr_ref[...].astype(F32) + zero) * state["y_ret"]
                  + _sigmoid(gc_ref[...].astype(F32) + zero) * state["y_conv"])
        h2_out(h1_ref[...] + gate * _dot(merged.astype(BF16), wout_ref[...]))

    return [conv_tail] + [head_tail(h) for h in range(RET_HEADS)] + [merge]


def _tail_kernel(h1_ref, modm_ref, o_ref_in, gret_ref, gr_ref, gc_ref, cv_ref, modf_ref,
                 gng_ref, gnb_ref, lng_ref, lnb_ref, wro_ref, wco_ref, bco_ref, wout_ref,
                 g3_ref, wgu_ref, wd_ref, gf_ref, out_ref, h2_ref, *, ck, final_norm):
    t = pl.program_id(0)

    @pl.when(t == 0)
    def _():
        h2_ref[...] = jnp.zeros(h2_ref.shape, F32)

    def h2_out(h2):
        h2_ref[t % 2] = h2

    side = _merge_pieces(h1_ref, modm_ref, o_ref_in, gret_ref, gr_ref, gc_ref, cv_ref, gng_ref, gnb_ref,
                         lng_ref, lnb_ref, wro_ref, wco_ref, bco_ref, wout_ref, h2_out)
    out_ref[...] = _ffn_math(h2_ref[(t + 1) % 2], modf_ref, g3_ref, wgu_ref, wd_ref, gf_ref,
                             sub=0, ck=ck, final_norm=final_norm, side=side)


def _tail(h1, mod3, o, gret, gr, gc, cv, gn_g, gn_b, ln_g, ln_b, w_ret_o, w_conv_o, b_conv_o, w_out,
          g_norm3, w_gu, w_d, g_final, *, final_norm, tm=512, ck=256):
    B, S, D = h1.shape
    V = o.shape[-1]
    Fh = w_d.shape[0]
    ntb = S // tm
    last_tile = B * ntb - 1

    def merged_tile(t):
        m = jnp.minimum(t, last_tile)
        return m // ntb, m % ntb

    def ffn_tile(t):
        f = jnp.maximum(t - 1, 0)
        return f // ntb, f % ntb

    row = lambda n: pl.BlockSpec((None, tm, n), lambda t: (*merged_tile(t), 0))
    mod_merge, mod_ffn = mod3[:, :, 3 * D:6 * D], mod3[:, :, 6 * D:9 * D]
    mod_block = lambda tile: pl.BlockSpec((None, 1, 3 * D), lambda t: (tile(t)[0], 0, 0))
    kern = functools.partial(_tail_kernel, ck=ck, final_norm=final_norm)
    return pl.pallas_call(
        kern,
        grid=(last_tile + 2,),
        in_specs=[row(D), mod_block(merged_tile), row(V), row(V), row(D), row(D), row(D), mod_block(ffn_tile),
                  _resident((1, V)), _resident((1, V)), _resident((1, D)), _resident((1, D)),
                  _resident((V, D)), _resident((D, D)), _resident((1, D)), _resident((D, D)),
                  _resident((1, D)), _resident((D, 2 * Fh)), _resident((Fh, D)), _resident((1, D))],
        out_specs=pl.BlockSpec((None, tm, D), lambda t: (*ffn_tile(t), 0)),
        out_shape=jax.ShapeDtypeStruct((B, S, D), F32),
        scratch_shapes=[pltpu.VMEM((2, tm, D), F32)],
        compiler_params=_params("arbitrary", vmem_limit_bytes=TAIL_VMEM_LIMIT_BYTES),
        name="tail",
    )(h1, mod_merge, o, gret, gr, gc, cv, mod_ffn, gn_g.reshape(1, V), gn_b.reshape(1, V), ln_g.reshape(1, D),
      ln_b.reshape(1, D), w_ret_o, w_conv_o, b_conv_o.reshape(1, D), w_out,
      g_norm3.reshape(1, D), w_gu, w_d, g_final.reshape(1, D))


def kernel(x, c, positions, w_mod, b_mod, g_norm1, w_ffn1_gu, w_ffn1_d, g_norm2, w_in, ret_gn_g, ret_gn_b,
           w_ret_o, w_dw, b_dw, conv_ln_g, conv_ln_b, w_conv_o, b_conv_o, w_out, g_norm3, w_ffn2_gu,
           w_ffn2_d, g_normf):
    B, S, D = x.shape
    depth = w_mod.shape[0]

    h = x
    for l in range(depth):
        last = l == depth - 1
        mod3 = _mod(c, w_mod[l], b_mod[l]).reshape(B, 1, 3 * N_SUB * D)
        h, (w_in_b, w_ro_b, w_co_b, w_out_b, w_gu2_b, w_d2_b) = _ffn(
            h, mod3, g_norm1[l], w_ffn1_gu[l], w_ffn1_d[l], g_normf,
            (w_in[l], w_ret_o[l], w_conv_o[l], w_out[l], w_ffn2_gu[l], w_ffn2_d[l]),
            sub=0, final_norm=False)
        q, k, v, gret, gr, gc, cv = _inproj(h, mod3, positions, g_norm2[l], w_in_b,
                                            w_dw[l].reshape(CONV_WIDTH, D), b_dw[l])
        o = _retention(q, k, v)
        h = _tail(h, mod3, o, gret, gr, gc, cv, ret_gn_g[l], ret_gn_b[l], conv_ln_g[l], conv_ln_b[l],
                  w_ro_b, w_co_b, b_conv_o[l], w_out_b,
                  g_norm3[l], w_gu2_b, w_d2_b, g_normf, final_norm=last)
    return h
```

```python
import functools

import numpy as np
import jax
import jax.numpy as jnp
from jax import lax
from jax.experimental import pallas as pl
from jax.experimental.pallas import tpu as pltpu

F32 = jnp.float32
BF16 = jnp.bfloat16

EPS = 1e-6
N_SUB = 3
RET_HEADS = 4
CONV_WIDTH = 31
ROPE_BASE = 10000.0
RET_CHUNK = 256
CONV_HALO = 32
LANES = 128
SUBLANES = 8
VMEM_LIMIT_BYTES = 56 * 1024 * 1024
TAIL_VMEM_LIMIT_BYTES = 62 * 1024 * 1024
INPROJ_VMEM_LIMIT_BYTES = 60 * 1024 * 1024


def _sigmoid(x):
    return 0.5 * jnp.tanh(0.5 * x) + 0.5


def _silu(x):
    h = 0.5 * x
    return h + h * jnp.tanh(h)


def _rms_norm(x, g):
    return x * lax.rsqrt(jnp.mean(x * x, axis=-1, keepdims=True) + EPS) * g


def _dot(a, b):
    return jnp.dot(a, b, preferred_element_type=F32)


def _resident(shape):
    zeros = (0,) * len(shape)
    return pl.BlockSpec(shape, lambda *_: zeros, pipeline_mode=pl.Buffered(1))


def _params(*semantics, vmem_limit_bytes=VMEM_LIMIT_BYTES):
    return pltpu.CompilerParams(dimension_semantics=semantics, vmem_limit_bytes=vmem_limit_bytes)


def _mod_kernel(c_ref, w_ref, b_ref, o_ref):
    c = c_ref[...]
    o_ref[...] = _dot(_silu(c).astype(BF16), w_ref[...].astype(BF16)) + b_ref[...]


def _mod(c, w_mod, b_mod, tn=1024):
    B, D = c.shape
    N = w_mod.shape[1]
    return pl.pallas_call(
        _mod_kernel,
        grid=(N // tn,),
        in_specs=[_resident((B, D)),
                  pl.BlockSpec((D, tn), lambda j: (0, j)),
                  pl.BlockSpec((1, tn), lambda j: (0, j))],
        out_specs=pl.BlockSpec((B, tn), lambda j: (0, j)),
        out_shape=jax.ShapeDtypeStruct((B, N), F32),
        compiler_params=_params("parallel"),
        name="mod",
    )(c, w_mod, b_mod.reshape(1, N))


def _mod_slices(mod_ref, sub, D):
    base = 3 * sub * D
    return (mod_ref[:, base:base + D], mod_ref[:, base + D:base + 2 * D],
            mod_ref[:, base + 2 * D:base + 3 * D])


def _bf16(w):
    return w if w.dtype == BF16 else w.astype(BF16)


def _zero_of(v):
    return v[0:1, 0:1] * 0.0


def _ffn_math(x, mod_ref, g_ref, wgu_ref, wd_ref, gf_ref, *, sub, ck, final_norm, side=()):
    D = x.shape[-1]
    Fh = wd_ref.shape[0]
    shift, scale, gate = _mod_slices(mod_ref, sub, D)
    xb = (_rms_norm(x, g_ref[...]) * (1.0 + scale) + shift).astype(BF16)
    acc = None
    zeros = []
    for c0 in range(0, Fh, ck):
        c1 = min(c0 + ck, Fh)
        g = _dot(xb, _bf16(wgu_ref[:, c0:c1]))
        u = _dot(xb, _bf16(wgu_ref[:, Fh + c0:Fh + c1]))
        zeros.append(_zero_of(g))
        part = _dot((_silu(g) * u).astype(BF16), _bf16(wd_ref[c0:c1, :]))
        acc = part if acc is None else acc + part
    h = x + (0.5 * gate) * acc
    if final_norm:
        h = _rms_norm(h, gf_ref[...])
    for piece, zero in zip(side, zeros):
        piece(zero)
    return h


def _ffn_kernel(x_ref, mod_ref, g_ref, wgu_ref, wd_ref, gf_ref, *rest, sub, ck, final_norm):
    n_cast = (len(rest) - 1) // 2
    o_ref = rest[n_cast]
    for src_ref, dst_ref in zip(rest[:n_cast], rest[n_cast + 1:]):
        dst_ref[...] = src_ref[...].astype(BF16)
    o_ref[...] = _ffn_math(x_ref[...], mod_ref, g_ref, wgu_ref, wd_ref, gf_ref,
                           sub=sub, ck=ck, final_norm=final_norm)


def _ffn(x, mod3, g_norm, w_gu, w_d, g_final, later_weights=(), *, sub, final_norm, tm=512, ck=256):
    B, S, D = x.shape
    Fh = w_d.shape[0]
    nt = S // tm
    steps = B * nt
    row = pl.BlockSpec((None, tm, D), lambda b, i: (b, i, 0))
    rides = [w.shape[0] % (steps * 2 * SUBLANES) == 0 for w in later_weights]
    riders = [w for w, r in zip(later_weights, rides) if r]
    cast_specs = [pl.BlockSpec((w.shape[0] // steps, w.shape[1]), lambda b, i: (b * nt + i, 0))
                  for w in riders]
    kern = functools.partial(_ffn_kernel, sub=sub, ck=ck, final_norm=final_norm)
    h, *cast = pl.pallas_call(
        kern,
        grid=(B, nt),
        in_specs=[row,
                  pl.BlockSpec((None, 1, mod3.shape[-1]), lambda b, i: (b, 0, 0)),
                  _resident((1, D)),
                  _resident((D, 2 * Fh)),
                  _resident((Fh, D)),
                  _resident((1, D))] + cast_specs,
        out_specs=[row] + cast_specs,
        out_shape=[jax.ShapeDtypeStruct((B, S, D), F32)]
                  + [jax.ShapeDtypeStruct(w.shape, BF16) for w in riders],
        compiler_params=_params("parallel", "parallel"),
        name=f"ffn{sub}",
    )(x, mod3, g_norm.reshape(1, D), w_gu, w_d, g_final.reshape(1, D), *riders)
    cast = iter(cast)
    return h, [next(cast) if r else w.astype(BF16) for w, r in zip(later_weights, rides)]


def _causal_conv_rows(win_ref, wdw_ref, bdw_ref, cv_ref, r0, rb, anchor):
    D = cv_ref.shape[-1]
    off0 = CONV_HALO - (CONV_WIDTH - 1)
    for c0 in range(0, D, LANES):
        wv = win_ref[c0 // LANES, r0:r0 + rb + CONV_HALO, :]
        acc = jnp.broadcast_to(bdw_ref[:, c0:c0 + LANES] + anchor, (rb, LANES))
        for m in range(SUBLANES):
            rows = rb if m == 0 else rb + SUBLANES
            g = None
            for p in range(m, off0 + CONV_WIDTH, SUBLANES):
                if p < off0:
                    continue
                term = wv[p - m:p - m + rows] * wdw_ref[p - off0:p - off0 + 1, c0:c0 + LANES]
                g = term if g is None else g + term
            acc = acc + g[m:m + rb]
        cv_ref[r0:r0 + rb, c0:c0 + LANES] = acc.astype(cv_ref.dtype)


def _rope_tables(contig_ref, pos_ref, freq_ref, ctab_ref, stab_ref, cos_ref, sin_ref):
    @pl.when(contig_ref[0, 0] != 0)
    def _():
        a0 = pos_ref[0:1, :].astype(F32) * freq_ref[...]
        c0, s0 = jnp.cos(a0), jnp.sin(a0)
        ct, st = ctab_ref[...], stab_ref[...]
        cos_ref[...] = c0 * ct - s0 * st
        sin_ref[...] = s0 * ct + c0 * st

    @pl.when(contig_ref[0, 0] == 0)
    def _():
        ang = pos_ref[...].astype(F32) * freq_ref[...]
        cos_ref[...] = jnp.cos(ang)
        sin_ref[...] = jnp.sin(ang)


def _retention_tables(chunk):
    log_gamma = np.log1p(-np.power(2.0, -5.0 - np.arange(RET_HEADS, dtype=np.float64)))
    idx = np.arange(chunk, dtype=np.float64)
    diff = idx[:, None] - idx[None, :]
    intra = np.where(diff[None] >= 0, np.exp(log_gamma[:, None, None] * np.maximum(diff, 0.0)[None]), 0.0)
    xi = np.exp(log_gamma[:, None] * (idx + 1.0))[:, :, None]
    zeta = np.exp(log_gamma[:, None] * (chunk - 1.0 - idx))[:, :, None]
    gamma_c = tuple(float(g) for g in np.exp(log_gamma * chunk))
    return (jnp.asarray(intra, F32), jnp.asarray(xi, F32), jnp.asarray(zeta, F32), gamma_c)


def _retention_chunk(q, k, v_halves, rows, intra_ref, xi_ref, zeta_ref, o_ref, state_ref, gamma_c):
    dk = q.shape[-1] // RET_HEADS
    dv = 2 * v_halves[0].shape[-1] // RET_HEADS
    per_half = RET_HEADS // 2
    first = None
    for h in range(RET_HEADS):
        qh = q[rows, h * dk:(h + 1) * dk]
        kh = k[rows, h * dk:(h + 1) * dk]
        vh = v_halves[h // per_half][rows, (h % per_half) * dv:(h % per_half + 1) * dv]
        s = lax.dot_general(qh, kh, (((1,), (1,)), ((), ())), preferred_element_type=F32) * intra_ref[h]
        st = state_ref[h]
        o = _dot(s.astype(BF16), vh) + _dot(qh, st.astype(BF16)) * xi_ref[h]
        o_ref[rows, h * dv:(h + 1) * dv] = o.astype(o_ref.dtype)
        kz = (kh.astype(F32) * zeta_ref[h]).astype(BF16)
        state_ref[h] = st * gamma_c[h] + lax.dot_general(
            kz, vh, (((0,), (0,)), ((), ())), preferred_element_type=F32)
        first = o if first is None else first
    return first


def _inproj_kernel(contig_ref, x_ref, mod_ref, pos_ref, freq_ref, ctab_ref, stab_ref, g_ref, w_ref, wdw_ref,
                   bdw_ref, intra_ref, xi_ref, zeta_ref, o_ref, gret_ref, gr_ref, gc_ref, cv_ref,
                   win_ref, cos_ref, sin_ref, state_ref, *, k_scale, gamma_c):
    tm, D = x_ref.shape
    dk = D // RET_HEADS
    half = dk // 2
    chunk = intra_ref.shape[-1]

    @pl.when(pl.program_id(1) == 0)
    def _():
        win_ref[:, 0:CONV_HALO, :] = jnp.zeros((D // LANES, CONV_HALO, LANES), F32)
        state_ref[...] = jnp.zeros(state_ref.shape, F32)

    _rope_tables(contig_ref, pos_ref, freq_ref, ctab_ref, stab_ref, cos_ref, sin_ref)

    x = x_ref[...]
    shift, scale, _ = _mod_slices(mod_ref, 1, D)
    xb = (_rms_norm(x, g_ref[...]) * (1.0 + scale) + shift).astype(BF16)

    def proj(c0):
        return _dot(xb, w_ref[:, c0:c0 + D])

    def anchor_of(z):
        return z[0:1, 0:LANES] * 0.0

    glu_a = proj(6 * D)
    glu_b = proj(7 * D)
    u = glu_a * _sigmoid(glu_b)
    for cb in range(D // LANES):
        win_ref[cb, CONV_HALO:CONV_HALO + tm, :] = u[:, cb * LANES:(cb + 1) * LANES]

    def rope(z, mult):
        cos = cos_ref[...]
        sin = sin_ref[...]
        parts = []
        for h in range(RET_HEADS):
            x1 = z[:, h * dk:h * dk + half]
            x2 = z[:, h * dk + half:(h + 1) * dk]
            parts.append(((x1 * cos - x2 * sin) * mult).astype(BF16))
            parts.append(((x1 * sin + x2 * cos) * mult).astype(BF16))
        return jnp.concatenate(parts, axis=1)

    n_conv = 6
    tiles = tm // (2 * SUBLANES)
    sizes = [2 * SUBLANES * (tiles // n_conv + (1 if j < tiles % n_conv else 0)) for j in range(n_conv)]
    starts = [sum(sizes[:j]) for j in range(n_conv)]

    def conv_block(j, z):
        _causal_conv_rows(win_ref, wdw_ref, bdw_ref, cv_ref, starts[j], sizes[j], anchor_of(z))

    q = rope(proj(0), 1.0)
    k = rope(proj(D), k_scale)
    zv0 = proj(2 * D)
    zv1 = proj(3 * D)
    conv_block(0, zv1)
    v_halves = (zv0.astype(BF16), zv1.astype(BF16))

    for c in range(tm // chunk):
        rows = slice(c * chunk, (c + 1) * chunk)
        first = _retention_chunk(q, k, v_halves, rows, intra_ref, xi_ref, zeta_ref, o_ref, state_ref, gamma_c)
        conv_block(1 + 2 * c, first)
        z = proj((4 + c) * D)
        conv_block(2 + 2 * c, z)
        gret_ref[:, c * D:(c + 1) * D] = z.astype(BF16)
    z = proj(8 * D)
    conv_block(5, z)
    gr_ref[...] = z.astype(BF16)
    gc_ref[...] = proj(9 * D).astype(BF16)

    win_ref[:, 0:CONV_HALO, :] = win_ref[:, tm:tm + CONV_HALO, :]


def _inproj(h, mod3, positions, g_norm, w_in, w_dw, b_dw, tm=512, chunk=RET_CHUNK):
    B, S, D = h.shape
    dk = D // RET_HEADS
    dv = 2 * dk
    half = dk // 2
    nt = S // tm
    assert tm // chunk == 2, "the stage plan interleaves exactly two retention chunks per tile"
    freqs = np.power(ROPE_BASE, -np.arange(0, half, dtype=np.float64) * (2.0 / dk))
    step = np.arange(tm, dtype=np.float64)[:, None] * freqs[None, :]
    ctab, stab = jnp.asarray(np.cos(step), F32), jnp.asarray(np.sin(step), F32)
    contig = jnp.all(positions[:, 1:] - positions[:, :-1] == 1).astype(jnp.int32).reshape(1, 1)
    intra, xi, zeta, gamma_c = _retention_tables(chunk)
    row = lambda n: pl.BlockSpec((None, tm, n), lambda b, i: (b, i, 0))
    out = lambda n: jax.ShapeDtypeStruct((B, S, n), BF16)
    kern = functools.partial(_inproj_kernel, k_scale=float(dk) ** -0.5, gamma_c=gamma_c)
    return pl.pallas_call(
        kern,
        grid=(B, nt),
        in_specs=[pl.BlockSpec(memory_space=pltpu.SMEM),
                  row(D),
                  pl.BlockSpec((None, 1, mod3.shape[-1]), lambda b, i: (b, 0, 0)),
                  row(1),
                  _resident((1, half)), _resident((tm, half)), _resident((tm, half)),
                  _resident((1, D)),
                  _resident(w_in.shape),
                  _resident((CONV_WIDTH, D)), _resident((1, D)),
                  _resident(intra.shape), _resident(xi.shape), _resident(zeta.shape)],
        out_specs=[row(2 * D), row(2 * D), row(D), row(D), row(D)],
        out_shape=[out(2 * D), out(2 * D), out(D), out(D), out(D)],
        scratch_shapes=[pltpu.VMEM((D // LANES, CONV_HALO + tm, LANES), F32), pltpu.VMEM((tm, half), F32),
                        pltpu.VMEM((tm, half), F32), pltpu.VMEM((RET_HEADS, dk, dv), F32)],
        compiler_params=_params("parallel", "arbitrary", vmem_limit_bytes=INPROJ_VMEM_LIMIT_BYTES),
        name="inproj",
    )(contig, h, mod3, positions.reshape(B, S, 1), jnp.asarray(freqs, F32).reshape(1, half), ctab, stab,
      g_norm.reshape(1, D), w_in, w_dw, b_dw.reshape(1, D), intra, xi, zeta)


def _merge_pieces(h1_ref, mod_ref, o_ref_in, gret_ref, gr_ref, gc_ref, cv_ref, gng_ref, gnb_ref,
                  lng_ref, lnb_ref, wro_ref, wco_ref, bco_ref, wout_ref, h2_out):
    D = h1_ref.shape[-1]
    dv = o_ref_in.shape[-1] // RET_HEADS
    state = {}

    def conv_tail(zero):
        cv = cv_ref[...].astype(F32)
        mu = jnp.mean(cv, axis=-1, keepdims=True)
        d = cv - mu
        var = jnp.mean(d * d, axis=-1, keepdims=True)
        ln = d * lax.rsqrt(var + EPS) * lng_ref[...] + (lnb_ref[...] + zero)
        state["y_conv"] = _dot(_silu(ln).astype(BF16), wco_ref[...]) + bco_ref[...]

    def head_tail(h):
        def run(zero):
            cols = slice(h * dv, (h + 1) * dv)
            o = o_ref_in[:, cols].astype(F32)
            mu = jnp.mean(o, axis=-1, keepdims=True)
            d = o - mu
            var = jnp.mean(d * d, axis=-1, keepdims=True)
            y = d * lax.rsqrt(var + EPS) * gng_ref[:, cols] + (gnb_ref[:, cols] + zero)
            og = (_silu(gret_ref[:, cols].astype(F32)) * y).astype(BF16)
            part = _dot(og, wro_ref[cols, :])
            state["y_ret"] = part if h == 0 else state["y_ret"] + part
        return run

    def merge(zero):
        _, _, gate = _mod_slices(mod_ref, 0, D)
        merged = (_sigmoid(gr_ref[...].astype(F32) + zero) * state["y_ret"]
                  + _sigmoid(gc_ref[...].astype(F32) + zero) * state["y_conv"])
        h2_out(h1_ref[...] + gate * _dot(merged.astype(BF16), wout_ref[...]))

    return [conv_tail] + [head_tail(h) for h in range(RET_HEADS)] + [merge]


def _tail_kernel(h1_ref, modm_ref, o_ref_in, gret_ref, gr_ref, gc_ref, cv_ref, modf_ref,
                 gng_ref, gnb_ref, lng_ref, lnb_ref, wro_ref, wco_ref, bco_ref, wout_ref,
                 g3_ref, wgu_ref, wd_ref, gf_ref, out_ref, h2_ref, *, ck, final_norm):
    t = pl.program_id(0)

    @pl.when(t == 0)
    def _():
        h2_ref[...] = jnp.zeros(h2_ref.shape, F32)

    def h2_out(h2):
        h2_ref[t % 2] = h2

    side = _merge_pieces(h1_ref, modm_ref, o_ref_in, gret_ref, gr_ref, gc_ref, cv_ref, gng_ref, gnb_ref,
                         lng_ref, lnb_ref, wro_ref, wco_ref, bco_ref, wout_ref, h2_out)
    out_ref[...] = _ffn_math(h2_ref[(t + 1) % 2], modf_ref, g3_ref, wgu_ref, wd_ref, gf_ref,
                             sub=0, ck=ck, final_norm=final_norm, side=side)


def _tail(h1, mod3, o, gret, gr, gc, cv, gn_g, gn_b, ln_g, ln_b, w_ret_o, w_conv_o, b_conv_o, w_out,
          g_norm3, w_gu, w_d, g_final, *, final_norm, tm=512, ck=256):
    B, S, D = h1.shape
    V = o.shape[-1]
    Fh = w_d.shape[0]
    ntb = S // tm
    last_tile = B * ntb - 1

    def merged_tile(t):
        m = jnp.minimum(t, last_tile)
        return m // ntb, m % ntb

    def ffn_tile(t):
        f = jnp.maximum(t - 1, 0)
        return f // ntb, f % ntb

    row = lambda n: pl.BlockSpec((None, tm, n), lambda t: (*merged_tile(t), 0))
    mod_merge, mod_ffn = mod3[:, :, 3 * D:6 * D], mod3[:, :, 6 * D:9 * D]
    mod_block = lambda tile: pl.BlockSpec((None, 1, 3 * D), lambda t: (tile(t)[0], 0, 0))
    kern = functools.partial(_tail_kernel, ck=ck, final_norm=final_norm)
    return pl.pallas_call(
        kern,
        grid=(last_tile + 2,),
        in_specs=[row(D), mod_block(merged_tile), row(V), row(V), row(D), row(D), row(D), mod_block(ffn_tile),
                  _resident((1, V)), _resident((1, V)), _resident((1, D)), _resident((1, D)),
                  _resident((V, D)), _resident((D, D)), _resident((1, D)), _resident((D, D)),
                  _resident((1, D)), _resident((D, 2 * Fh)), _resident((Fh, D)), _resident((1, D))],
        out_specs=pl.BlockSpec((None, tm, D), lambda t: (*ffn_tile(t), 0)),
        out_shape=jax.ShapeDtypeStruct((B, S, D), F32),
        scratch_shapes=[pltpu.VMEM((2, tm, D), F32)],
        compiler_params=_params("arbitrary", vmem_limit_bytes=TAIL_VMEM_LIMIT_BYTES),
        name="tail",
    )(h1, mod_merge, o, gret, gr, gc, cv, mod_ffn, gn_g.reshape(1, V), gn_b.reshape(1, V), ln_g.reshape(1, D),
      ln_b.reshape(1, D), w_ret_o, w_conv_o, b_conv_o.reshape(1, D), w_out,
      g_norm3.reshape(1, D), w_gu, w_d, g_final.reshape(1, D))


def kernel(x, c, positions, w_mod, b_mod, g_norm1, w_ffn1_gu, w_ffn1_d, g_norm2, w_in, ret_gn_g, ret_gn_b,
           w_ret_o, w_dw, b_dw, conv_ln_g, conv_ln_b, w_conv_o, b_conv_o, w_out, g_norm3, w_ffn2_gu,
           w_ffn2_d, g_normf):
    B, S, D = x.shape
    depth = w_mod.shape[0]

    h = x
    for l in range(depth):
        last = l == depth - 1
        mod3 = _mod(c, w_mod[l], b_mod[l]).reshape(B, 1, 3 * N_SUB * D)
        h, (w_in_b, w_ro_b, w_co_b, w_out_b, w_gu2_b, w_d2_b) = _ffn(
            h, mod3, g_norm1[l], w_ffn1_gu[l], w_ffn1_d[l], g_normf,
            (w_in[l], w_ret_o[l], w_conv_o[l], w_out[l], w_ffn2_gu[l], w_ffn2_d[l]),
            sub=0, final_norm=False)
        o, gret, gr, gc, cv = _inproj(h, mod3, positions, g_norm2[l], w_in_b,
                                      w_dw[l].reshape(CONV_WIDTH, D), b_dw[l])
        h = _tail(h, mod3, o, gret, gr, gc, cv, ret_gn_g[l], ret_gn_b[l], conv_ln_g[l], conv_ln_b[l],
                  w_ro_b, w_co_b, b_conv_o[l], w_out_b,
                  g_norm3[l], w_gu2_b, w_d2_b, g_normf, final_norm=last)
    return h
```

```python
import functools

import numpy as np
import jax
import jax.numpy as jnp
from jax import lax
from jax.experimental import pallas as pl
from jax.experimental.pallas import tpu as pltpu

F32 = jnp.float32
BF16 = jnp.bfloat16

EPS = 1e-6
N_SUB = 3
RET_HEADS = 4
CONV_WIDTH = 31
ROPE_BASE = 10000.0
RET_CHUNK = 256
CONV_HALO = 32
LANES = 128
SUBLANES = 8
VMEM_LIMIT_BYTES = 56 * 1024 * 1024
TAIL_VMEM_LIMIT_BYTES = 62 * 1024 * 1024
INPROJ_VMEM_LIMIT_BYTES = 60 * 1024 * 1024


def _sigmoid(x):
    return 0.5 * jnp.tanh(0.5 * x) + 0.5


def _silu(x):
    h = 0.5 * x
    return h + h * jnp.tanh(h)


def _rms_norm(x, g):
    return x * lax.rsqrt(jnp.mean(x * x, axis=-1, keepdims=True) + EPS) * g


def _dot(a, b):
    return jnp.dot(a, b, preferred_element_type=F32)


def _resident(shape):
    zeros = (0,) * len(shape)
    return pl.BlockSpec(shape, lambda *_: zeros, pipeline_mode=pl.Buffered(1))


def _params(*semantics, vmem_limit_bytes=VMEM_LIMIT_BYTES):
    return pltpu.CompilerParams(dimension_semantics=semantics, vmem_limit_bytes=vmem_limit_bytes)


def _mod_kernel(c_ref, w_ref, b_ref, o_ref):
    c = c_ref[...]
    o_ref[...] = _dot(_silu(c).astype(BF16), w_ref[...].astype(BF16)) + b_ref[...]


def _mod(c, w_mod, b_mod, tn=1024):
    B, D = c.shape
    N = w_mod.shape[1]
    return pl.pallas_call(
        _mod_kernel,
        grid=(N // tn,),
        in_specs=[_resident((B, D)),
                  pl.BlockSpec((D, tn), lambda j: (0, j)),
                  pl.BlockSpec((1, tn), lambda j: (0, j))],
        out_specs=pl.BlockSpec((B, tn), lambda j: (0, j)),
        out_shape=jax.ShapeDtypeStruct((B, N), F32),
        compiler_params=_params("parallel"),
        name="mod",
    )(c, w_mod, b_mod.reshape(1, N))


def _mod_slices(mod_ref, sub, D):
    base = 3 * sub * D
    return (mod_ref[:, base:base + D], mod_ref[:, base + D:base + 2 * D],
            mod_ref[:, base + 2 * D:base + 3 * D])


def _bf16(w):
    return w if w.dtype == BF16 else w.astype(BF16)


def _zero_of(v):
    return v[0:1, 0:1] * 0.0


def _ffn_math(x, mod_ref, g_ref, wgu_ref, wd_ref, gf_ref, *, sub, ck, final_norm, side=()):
    D = x.shape[-1]
    Fh = wd_ref.shape[0]
    shift, scale, gate = _mod_slices(mod_ref, sub, D)
    xb = (_rms_norm(x, g_ref[...] * (1.0 + scale)) + shift).astype(BF16)
    acc = None
    zeros = []
    for c0 in range(0, Fh, ck):
        c1 = min(c0 + ck, Fh)
        g = _dot(xb, _bf16(wgu_ref[:, c0:c1]))
        u = _dot(xb, _bf16(wgu_ref[:, Fh + c0:Fh + c1]))
        zeros.append(_zero_of(g))
        part = _dot((_silu(g) * u).astype(BF16), _bf16(wd_ref[c0:c1, :]))
        acc = part if acc is None else acc + part
    h = x + (0.5 * gate) * acc
    if final_norm:
        h = _rms_norm(h, gf_ref[...])
    for piece, zero in zip(side, zeros):
        piece(zero)
    return h


def _ffn_kernel(x_ref, mod_ref, g_ref, wgu_ref, wd_ref, gf_ref, *rest, sub, ck, final_norm):
    n_cast = (len(rest) - 1) // 2
    o_ref = rest[n_cast]
    for src_ref, dst_ref in zip(rest[:n_cast], rest[n_cast + 1:]):
        dst_ref[...] = src_ref[...].astype(BF16)
    o_ref[...] = _ffn_math(x_ref[...], mod_ref, g_ref, wgu_ref, wd_ref, gf_ref,
                           sub=sub, ck=ck, final_norm=final_norm)


def _ffn(x, mod3, g_norm, w_gu, w_d, g_final, later_weights=(), *, sub, final_norm, tm=512, ck=256):
    B, S, D = x.shape
    Fh = w_d.shape[0]
    nt = S // tm
    steps = B * nt
    row = pl.BlockSpec((None, tm, D), lambda b, i: (b, i, 0))
    rides = [w.shape[0] % (steps * 2 * SUBLANES) == 0 for w in later_weights]
    riders = [w for w, r in zip(later_weights, rides) if r]
    cast_specs = [pl.BlockSpec((w.shape[0] // steps, w.shape[1]), lambda b, i: (b * nt + i, 0))
                  for w in riders]
    kern = functools.partial(_ffn_kernel, sub=sub, ck=ck, final_norm=final_norm)
    h, *cast = pl.pallas_call(
        kern,
        grid=(B, nt),
        in_specs=[row,
                  pl.BlockSpec((None, 1, mod3.shape[-1]), lambda b, i: (b, 0, 0)),
                  _resident((1, D)),
                  _resident((D, 2 * Fh)),
                  _resident((Fh, D)),
                  _resident((1, D))] + cast_specs,
        out_specs=[row] + cast_specs,
        out_shape=[jax.ShapeDtypeStruct((B, S, D), F32)]
                  + [jax.ShapeDtypeStruct(w.shape, BF16) for w in riders],
        compiler_params=_params("parallel", "parallel"),
        name=f"ffn{sub}",
    )(x, mod3, g_norm.reshape(1, D), w_gu, w_d, g_final.reshape(1, D), *riders)
    cast = iter(cast)
    return h, [next(cast) if r else w.astype(BF16) for w, r in zip(later_weights, rides)]


def _causal_conv_rows(win_ref, wdw_ref, bdw_ref, cv_ref, r0, rb, anchor):
    D = cv_ref.shape[-1]
    off0 = CONV_HALO - (CONV_WIDTH - 1)
    for c0 in range(0, D, LANES):
        wv = win_ref[c0 // LANES, r0:r0 + rb + CONV_HALO, :]
        acc = jnp.broadcast_to(bdw_ref[:, c0:c0 + LANES] + anchor, (rb, LANES))
        for m in range(SUBLANES):
            rows = rb if m == 0 else rb + SUBLANES
            g = None
            for p in range(m, off0 + CONV_WIDTH, SUBLANES):
                if p < off0:
                    continue
                term = wv[p - m:p - m + rows] * wdw_ref[p - off0:p - off0 + 1, c0:c0 + LANES]
                g = term if g is None else g + term
            acc = acc + g[m:m + rb]
        cv_ref[r0:r0 + rb, c0:c0 + LANES] = acc.astype(cv_ref.dtype)


def _rope_tables(contig_ref, pos_ref, freq_ref, ctab_ref, stab_ref, cos_ref, sin_ref):
    @pl.when(contig_ref[0, 0] != 0)
    def _():
        a0 = pos_ref[0:1, :].astype(F32) * freq_ref[...]
        c0, s0 = jnp.cos(a0), jnp.sin(a0)
        ct, st = ctab_ref[...], stab_ref[...]
        cos_ref[...] = c0 * ct - s0 * st
        sin_ref[...] = s0 * ct + c0 * st

    @pl.when(contig_ref[0, 0] == 0)
    def _():
        ang = pos_ref[...].astype(F32) * freq_ref[...]
        cos_ref[...] = jnp.cos(ang)
        sin_ref[...] = jnp.sin(ang)


def _retention_tables(chunk):
    log_gamma = np.log1p(-np.power(2.0, -5.0 - np.arange(RET_HEADS, dtype=np.float64)))
    idx = np.arange(chunk, dtype=np.float64)
    diff = idx[:, None] - idx[None, :]
    intra = np.where(diff[None] >= 0, np.exp(log_gamma[:, None, None] * np.maximum(diff, 0.0)[None]), 0.0)
    xi = np.exp(log_gamma[:, None] * (idx + 1.0))[:, :, None]
    zeta = np.exp(log_gamma[:, None] * (chunk - 1.0 - idx))[:, :, None]
    gamma_c = tuple(float(g) for g in np.exp(log_gamma * chunk))
    return (jnp.asarray(intra, F32), jnp.asarray(xi, F32), jnp.asarray(zeta, F32), gamma_c)


def _retention_chunk(q, k, v_halves, rows, intra_ref, xi_ref, zeta_ref, o_ref, state_ref, gamma_c):
    dk = q.shape[-1] // RET_HEADS
    dv = 2 * v_halves[0].shape[-1] // RET_HEADS
    per_half = RET_HEADS // 2
    first = None
    for h in range(RET_HEADS):
        qh = q[rows, h * dk:(h + 1) * dk]
        kh = k[rows, h * dk:(h + 1) * dk]
        vh = v_halves[h // per_half][rows, (h % per_half) * dv:(h % per_half + 1) * dv]
        s = lax.dot_general(qh, kh, (((1,), (1,)), ((), ())), preferred_element_type=F32) * intra_ref[h]
        st = state_ref[h]
        o = _dot(s.astype(BF16), vh) + _dot(qh, st.astype(BF16)) * xi_ref[h]
        o_ref[rows, h * dv:(h + 1) * dv] = o.astype(o_ref.dtype)
        kz = (kh.astype(F32) * zeta_ref[h]).astype(BF16)
        state_ref[h] = st * gamma_c[h] + lax.dot_general(
            kz, vh, (((0,), (0,)), ((), ())), preferred_element_type=F32)
        first = o if first is None else first
    return first


def _inproj_kernel(contig_ref, x_ref, mod_ref, pos_ref, freq_ref, ctab_ref, stab_ref, g_ref, w_ref, wdw_ref,
                   bdw_ref, intra_ref, xi_ref, zeta_ref, o_ref, gret_ref, gr_ref, gc_ref, cv_ref,
                   win_ref, cos_ref, sin_ref, state_ref, *, k_scale, gamma_c):
    tm, D = x_ref.shape
    dk = D // RET_HEADS
    half = dk // 2
    chunk = intra_ref.shape[-1]

    @pl.when(pl.program_id(1) == 0)
    def _():
        win_ref[:, 0:CONV_HALO, :] = jnp.zeros((D // LANES, CONV_HALO, LANES), F32)
        state_ref[...] = jnp.zeros(state_ref.shape, F32)

    _rope_tables(contig_ref, pos_ref, freq_ref, ctab_ref, stab_ref, cos_ref, sin_ref)

    x = x_ref[...]
    shift, scale, _ = _mod_slices(mod_ref, 1, D)
    xb = (_rms_norm(x, g_ref[...] * (1.0 + scale)) + shift).astype(BF16)

    def proj(c0):
        return _dot(xb, w_ref[:, c0:c0 + D])

    def anchor_of(z):
        return z[0:1, 0:LANES] * 0.0

    glu_a = proj(6 * D)
    glu_b = proj(7 * D)
    u = glu_a * _sigmoid(glu_b)
    for cb in range(D // LANES):
        win_ref[cb, CONV_HALO:CONV_HALO + tm, :] = u[:, cb * LANES:(cb + 1) * LANES]

    def rope(z, mult):
        cos = cos_ref[...]
        sin = sin_ref[...]
        parts = []
        for h in range(RET_HEADS):
            x1 = z[:, h * dk:h * dk + half]
            x2 = z[:, h * dk + half:(h + 1) * dk]
            parts.append(((x1 * cos - x2 * sin) * mult).astype(BF16))
            parts.append(((x1 * sin + x2 * cos) * mult).astype(BF16))
        return jnp.concatenate(parts, axis=1)

    n_conv = 6
    tiles = tm // (2 * SUBLANES)
    sizes = [2 * SUBLANES * (tiles // n_conv + (1 if j < tiles % n_conv else 0)) for j in range(n_conv)]
    starts = [sum(sizes[:j]) for j in range(n_conv)]

    def conv_block(j, z):
        _causal_conv_rows(win_ref, wdw_ref, bdw_ref, cv_ref, starts[j], sizes[j], anchor_of(z))

    q = rope(proj(0), 1.0)
    k = rope(proj(D), k_scale)
    zv0 = proj(2 * D)
    zv1 = proj(3 * D)
    conv_block(0, zv1)
    v_halves = (zv0.astype(BF16), zv1.astype(BF16))

    for c in range(tm // chunk):
        rows = slice(c * chunk, (c + 1) * chunk)
        first = _retention_chunk(q, k, v_halves, rows, intra_ref, xi_ref, zeta_ref, o_ref, state_ref, gamma_c)
        conv_block(1 + 2 * c, first)
        z = proj((4 + c) * D)
        conv_block(2 + 2 * c, z)
        gret_ref[:, c * D:(c + 1) * D] = z.astype(BF16)
    z = proj(8 * D)
    conv_block(5, z)
    gr_ref[...] = z.astype(BF16)
    gc_ref[...] = proj(9 * D).astype(BF16)

    win_ref[:, 0:CONV_HALO, :] = win_ref[:, tm:tm + CONV_HALO, :]


def _inproj(h, mod3, positions, g_norm, w_in, w_dw, b_dw, tm=512, chunk=RET_CHUNK):
    B, S, D = h.shape
    dk = D // RET_HEADS
    dv = 2 * dk
    half = dk // 2
    nt = S // tm
    assert tm // chunk == 2, "the stage plan interleaves exactly two retention chunks per tile"
    freqs = np.power(ROPE_BASE, -np.arange(0, half, dtype=np.float64) * (2.0 / dk))
    step = np.arange(tm, dtype=np.float64)[:, None] * freqs[None, :]
    ctab, stab = jnp.asarray(np.cos(step), F32), jnp.asarray(np.sin(step), F32)
    contig = jnp.all(positions[:, 1:] - positions[:, :-1] == 1).astype(jnp.int32).reshape(1, 1)
    intra, xi, zeta, gamma_c = _retention_tables(chunk)
    row = lambda n: pl.BlockSpec((None, tm, n), lambda b, i: (b, i, 0))
    out = lambda n: jax.ShapeDtypeStruct((B, S, n), BF16)
    kern = functools.partial(_inproj_kernel, k_scale=float(dk) ** -0.5, gamma_c=gamma_c)
    return pl.pallas_call(
        kern,
        grid=(B, nt),
        in_specs=[pl.BlockSpec(memory_space=pltpu.SMEM),
                  row(D),
                  pl.BlockSpec((None, 1, mod3.shape[-1]), lambda b, i: (b, 0, 0)),
                  row(1),
                  _resident((1, half)), _resident((tm, half)), _resident((tm, half)),
                  _resident((1, D)),
                  _resident(w_in.shape),
                  _resident((CONV_WIDTH, D)), _resident((1, D)),
                  _resident(intra.shape), _resident(xi.shape), _resident(zeta.shape)],
        out_specs=[row(2 * D), row(2 * D), row(D), row(D), row(D)],
        out_shape=[out(2 * D), out(2 * D), out(D), out(D), out(D)],
        scratch_shapes=[pltpu.VMEM((D // LANES, CONV_HALO + tm, LANES), F32), pltpu.VMEM((tm, half), F32),
                        pltpu.VMEM((tm, half), F32), pltpu.VMEM((RET_HEADS, dk, dv), F32)],
        compiler_params=_params("parallel", "arbitrary", vmem_limit_bytes=INPROJ_VMEM_LIMIT_BYTES),
        name="inproj",
    )(contig, h, mod3, positions.reshape(B, S, 1), jnp.asarray(freqs, F32).reshape(1, half), ctab, stab,
      g_norm.reshape(1, D), w_in, w_dw, b_dw.reshape(1, D), intra, xi, zeta)


def _merge_pieces(h1_ref, mod_ref, o_ref_in, gret_ref, gr_ref, gc_ref, cv_ref, gng_ref, gnb_ref,
                  lng_ref, lnb_ref, wro_ref, wco_ref, bco_ref, wout_ref, h2_out):
    D = h1_ref.shape[-1]
    dv = o_ref_in.shape[-1] // RET_HEADS
    state = {}

    def conv_tail(zero):
        cv = cv_ref[...].astype(F32)
        mu = jnp.mean(cv, axis=-1, keepdims=True)
        d = cv - mu
        var = jnp.mean(d * d, axis=-1, keepdims=True)
        ln = d * lax.rsqrt(var + EPS) * lng_ref[...] + (lnb_ref[...] + zero)
        state["y_conv"] = _dot(_silu(ln).astype(BF16), wco_ref[...]) + bco_ref[...]

    def head_tail(h):
        def run(zero):
            cols = slice(h * dv, (h + 1) * dv)
            o = o_ref_in[:, cols].astype(F32)
            mu = jnp.mean(o, axis=-1, keepdims=True)
            d = o - mu
            var = jnp.mean(d * d, axis=-1, keepdims=True)
            y = d * lax.rsqrt(var + EPS) * gng_ref[:, cols] + (gnb_ref[:, cols] + zero)
            og = (_silu(gret_ref[:, cols].astype(F32)) * y).astype(BF16)
            part = _dot(og, wro_ref[cols, :])
            state["y_ret"] = part if h == 0 else state["y_ret"] + part
        return run

    def merge(zero):
        _, _, gate = _mod_slices(mod_ref, 0, D)
        merged = (_sigmoid(gr_ref[...].astype(F32) + zero) * state["y_ret"]
                  + _sigmoid(gc_ref[...].astype(F32) + zero) * state["y_conv"])
        h2_out(h1_ref[...] + gate * _dot(merged.astype(BF16), wout_ref[...]))

    return [conv_tail] + [head_tail(h) for h in range(RET_HEADS)] + [merge]


def _tail_kernel(h1_ref, modm_ref, o_ref_in, gret_ref, gr_ref, gc_ref, cv_ref, modf_ref,
                 gng_ref, gnb_ref, lng_ref, lnb_ref, wro_ref, wco_ref, bco_ref, wout_ref,
                 g3_ref, wgu_ref, wd_ref, gf_ref, out_ref, h2_ref, *, ck, final_norm):
    t = pl.program_id(0)

    @pl.when(t == 0)
    def _():
        h2_ref[...] = jnp.zeros(h2_ref.shape, F32)

    def h2_out(h2):
        h2_ref[t % 2] = h2

    side = _merge_pieces(h1_ref, modm_ref, o_ref_in, gret_ref, gr_ref, gc_ref, cv_ref, gng_ref, gnb_ref,
                         lng_ref, lnb_ref, wro_ref, wco_ref, bco_ref, wout_ref, h2_out)
    out_ref[...] = _ffn_math(h2_ref[(t + 1) % 2], modf_ref, g3_ref, wgu_ref, wd_ref, gf_ref,
                             sub=0, ck=ck, final_norm=final_norm, side=side)


def _tail(h1, mod3, o, gret, gr, gc, cv, gn_g, gn_b, ln_g, ln_b, w_ret_o, w_conv_o, b_conv_o, w_out,
          g_norm3, w_gu, w_d, g_final, *, final_norm, tm=512, ck=256):
    B, S, D = h1.shape
    V = o.shape[-1]
    Fh = w_d.shape[0]
    ntb = S // tm
    last_tile = B * ntb - 1

    def merged_tile(t):
        m = jnp.minimum(t, last_tile)
        return m // ntb, m % ntb

    def ffn_tile(t):
        f = jnp.maximum(t - 1, 0)
        return f // ntb, f % ntb

    row = lambda n: pl.BlockSpec((None, tm, n), lambda t: (*merged_tile(t), 0))
    mod_merge, mod_ffn = mod3[:, :, 3 * D:6 * D], mod3[:, :, 6 * D:9 * D]
    mod_block = lambda tile: pl.BlockSpec((None, 1, 3 * D), lambda t: (tile(t)[0], 0, 0))
    kern = functools.partial(_tail_kernel, ck=ck, final_norm=final_norm)
    return pl.pallas_call(
        kern,
        grid=(last_tile + 2,),
        in_specs=[row(D), mod_block(merged_tile), row(V), row(V), row(D), row(D), row(D), mod_block(ffn_tile),
                  _resident((1, V)), _resident((1, V)), _resident((1, D)), _resident((1, D)),
                  _resident((V, D)), _resident((D, D)), _resident((1, D)), _resident((D, D)),
                  _resident((1, D)), _resident((D, 2 * Fh)), _resident((Fh, D)), _resident((1, D))],
        out_specs=pl.BlockSpec((None, tm, D), lambda t: (*ffn_tile(t), 0)),
        out_shape=jax.ShapeDtypeStruct((B, S, D), F32),
        scratch_shapes=[pltpu.VMEM((2, tm, D), F32)],
        compiler_params=_params("arbitrary", vmem_limit_bytes=TAIL_VMEM_LIMIT_BYTES),
        name="tail",
    )(h1, mod_merge, o, gret, gr, gc, cv, mod_ffn, gn_g.reshape(1, V), gn_b.reshape(1, V), ln_g.reshape(1, D),
      ln_b.reshape(1, D), w_ret_o, w_conv_o, b_conv_o.reshape(1, D), w_out,
      g_norm3.reshape(1, D), w_gu, w_d, g_final.reshape(1, D))


def kernel(x, c, positions, w_mod, b_mod, g_norm1, w_ffn1_gu, w_ffn1_d, g_norm2, w_in, ret_gn_g, ret_gn_b,
           w_ret_o, w_dw, b_dw, conv_ln_g, conv_ln_b, w_conv_o, b_conv_o, w_out, g_norm3, w_ffn2_gu,
           w_ffn2_d, g_normf):
    B, S, D = x.shape
    depth = w_mod.shape[0]

    h = x
    for l in range(depth):
        last = l == depth - 1
        mod3 = _mod(c, w_mod[l], b_mod[l]).reshape(B, 1, 3 * N_SUB * D)
        h, (w_in_b, w_ro_b, w_co_b, w_out_b, w_gu2_b, w_d2_b) = _ffn(
            h, mod3, g_norm1[l], w_ffn1_gu[l], w_ffn1_d[l], g_normf,
            (w_in[l], w_ret_o[l], w_conv_o[l], w_out[l], w_ffn2_gu[l], w_ffn2_d[l]),
            sub=0, final_norm=False)
        o, gret, gr, gc, cv = _inproj(h, mod3, positions, g_norm2[l], w_in_b,
                                      w_dw[l].reshape(CONV_WIDTH, D), b_dw[l])
        h = _tail(h, mod3, o, gret, gr, gc, cv, ret_gn_g[l], ret_gn_b[l], conv_ln_g[l], conv_ln_b[l],
                  w_ro_b, w_co_b, b_conv_o[l], w_out_b,
                  g_norm3[l], w_gu2_b, w_d2_b, g_normf, final_norm=last)
    return h
```

```python
import functools

import numpy as np
import jax
import jax.numpy as jnp
from jax import lax
from jax.experimental import pallas as pl
from jax.experimental.pallas import tpu as pltpu

F32 = jnp.float32
BF16 = jnp.bfloat16

EPS = 1e-6
N_SUB = 3
RET_HEADS = 4
CONV_WIDTH = 31
ROPE_BASE = 10000.0
RET_CHUNK = 256
CONV_HALO = 32
LANES = 128
SUBLANES = 8
VMEM_LIMIT_BYTES = 56 * 1024 * 1024
TAIL_VMEM_LIMIT_BYTES = 62 * 1024 * 1024
INPROJ_VMEM_LIMIT_BYTES = 60 * 1024 * 1024


def _sigmoid(x):
    return 0.5 * jnp.tanh(0.5 * x) + 0.5


def _silu(x):
    h = 0.5 * x
    return h + h * jnp.tanh(h)


def _rms_norm(x, g):
    return x * lax.rsqrt(jnp.mean(x * x, axis=-1, keepdims=True) + EPS) * g


def _dot(a, b):
    return jnp.dot(a, b, preferred_element_type=F32)


def _resident(shape):
    zeros = (0,) * len(shape)
    return pl.BlockSpec(shape, lambda *_: zeros, pipeline_mode=pl.Buffered(1))


def _params(*semantics, vmem_limit_bytes=VMEM_LIMIT_BYTES):
    return pltpu.CompilerParams(dimension_semantics=semantics, vmem_limit_bytes=vmem_limit_bytes)


def _mod_kernel(c_ref, w_ref, b_ref, o_ref):
    c = c_ref[...]
    o_ref[...] = _dot(_silu(c).astype(BF16), w_ref[...].astype(BF16)) + b_ref[...]


def _mod(c, w_mod, b_mod, tn=1024):
    B, D = c.shape
    N = w_mod.shape[1]
    return pl.pallas_call(
        _mod_kernel,
        grid=(N // tn,),
        in_specs=[_resident((B, D)),
                  pl.BlockSpec((D, tn), lambda j: (0, j)),
                  pl.BlockSpec((1, tn), lambda j: (0, j))],
        out_specs=pl.BlockSpec((B, tn), lambda j: (0, j)),
        out_shape=jax.ShapeDtypeStruct((B, N), F32),
        compiler_params=_params("parallel"),
        name="mod",
    )(c, w_mod, b_mod.reshape(1, N))


def _mod_slices(mod_ref, sub, D):
    base = 3 * sub * D
    return (mod_ref[:, base:base + D], mod_ref[:, base + D:base + 2 * D],
            mod_ref[:, base + 2 * D:base + 3 * D])


def _bf16(w):
    return w if w.dtype == BF16 else w.astype(BF16)


def _zero_of(v):
    return v[0:1, 0:1] * 0.0


def _ffn_math(x, mod_ref, g_ref, wgu_ref, wd_ref, gf_ref, *, sub, ck, final_norm, side=()):
    D = x.shape[-1]
    Fh = wd_ref.shape[0]
    shift, scale, gate = _mod_slices(mod_ref, sub, D)
    xb = (_rms_norm(x, g_ref[...] * (1.0 + scale)) + shift).astype(BF16)
    acc = None
    zeros = []
    for c0 in range(0, Fh, ck):
        c1 = min(c0 + ck, Fh)
        g = _dot(xb, _bf16(wgu_ref[:, c0:c1]))
        u = _dot(xb, _bf16(wgu_ref[:, Fh + c0:Fh + c1]))
        zeros.append(_zero_of(g))
        part = _dot((_silu(g) * u).astype(BF16), _bf16(wd_ref[c0:c1, :]))
        acc = part if acc is None else acc + part
    h = x + (0.5 * gate) * acc
    if final_norm:
        h = _rms_norm(h, gf_ref[...])
    for piece, zero in zip(side, zeros):
        piece(zero)
    return h


def _ffn_kernel(x_ref, mod_ref, g_ref, wgu_ref, wd_ref, gf_ref, *rest, sub, ck, final_norm):
    n_cast = (len(rest) - 1) // 2
    o_ref = rest[n_cast]
    for src_ref, dst_ref in zip(rest[:n_cast], rest[n_cast + 1:]):
        dst_ref[...] = src_ref[...].astype(BF16)
    o_ref[...] = _ffn_math(x_ref[...], mod_ref, g_ref, wgu_ref, wd_ref, gf_ref,
                           sub=sub, ck=ck, final_norm=final_norm)


def _ffn(x, mod3, g_norm, w_gu, w_d, g_final, later_weights=(), *, sub, final_norm, tm=512, ck=256):
    B, S, D = x.shape
    Fh = w_d.shape[0]
    nt = S // tm
    steps = B * nt
    row = pl.BlockSpec((None, tm, D), lambda b, i: (b, i, 0))
    rides = [w.shape[0] % (steps * 2 * SUBLANES) == 0 for w in later_weights]
    riders = [w for w, r in zip(later_weights, rides) if r]
    cast_specs = [pl.BlockSpec((w.shape[0] // steps, w.shape[1]), lambda b, i: (b * nt + i, 0))
                  for w in riders]
    kern = functools.partial(_ffn_kernel, sub=sub, ck=ck, final_norm=final_norm)
    h, *cast = pl.pallas_call(
        kern,
        grid=(B, nt),
        in_specs=[row,
                  pl.BlockSpec((None, 1, mod3.shape[-1]), lambda b, i: (b, 0, 0)),
                  _resident((1, D)),
                  _resident((D, 2 * Fh)),
                  _resident((Fh, D)),
                  _resident((1, D))] + cast_specs,
        out_specs=[row] + cast_specs,
        out_shape=[jax.ShapeDtypeStruct((B, S, D), F32)]
                  + [jax.ShapeDtypeStruct(w.shape, BF16) for w in riders],
        compiler_params=_params("parallel", "parallel"),
        name=f"ffn{sub}",
    )(x, mod3, g_norm.reshape(1, D), w_gu, w_d, g_final.reshape(1, D), *riders)
    cast = iter(cast)
    return h, [next(cast) if r else w.astype(BF16) for w, r in zip(later_weights, rides)]


def _causal_conv_rows(win_ref, wdw_ref, bdw_ref, cv_ref, r0, rb, anchor):
    D = cv_ref.shape[-1]
    off0 = CONV_HALO - (CONV_WIDTH - 1)
    for c0 in range(0, D, LANES):
        wv = win_ref[c0 // LANES, r0:r0 + rb + CONV_HALO, :]
        acc = jnp.broadcast_to(bdw_ref[:, c0:c0 + LANES] + anchor, (rb, LANES))
        for m in range(SUBLANES):
            rows = rb if m == 0 else rb + SUBLANES
            g = None
            for p in range(m, off0 + CONV_WIDTH, SUBLANES):
                if p < off0:
                    continue
                term = wv[p - m:p - m + rows] * wdw_ref[p - off0:p - off0 + 1, c0:c0 + LANES]
                g = term if g is None else g + term
            acc = acc + g[m:m + rb]
        cv_ref[r0:r0 + rb, c0:c0 + LANES] = acc.astype(cv_ref.dtype)


def _rope_tables(contig_ref, pos_ref, freq_ref, ctab_ref, stab_ref, cos_ref, sin_ref):
    @pl.when(contig_ref[0, 0] != 0)
    def _():
        a0 = pos_ref[0:1, :].astype(F32) * freq_ref[...]
        c0, s0 = jnp.cos(a0), jnp.sin(a0)
        ct, st = ctab_ref[...], stab_ref[...]
        cos_ref[...] = c0 * ct - s0 * st
        sin_ref[...] = s0 * ct + c0 * st

    @pl.when(contig_ref[0, 0] == 0)
    def _():
        ang = pos_ref[...].astype(F32) * freq_ref[...]
        cos_ref[...] = jnp.cos(ang)
        sin_ref[...] = jnp.sin(ang)


def _retention_tables(chunk):
    log_gamma = np.log1p(-np.power(2.0, -5.0 - np.arange(RET_HEADS, dtype=np.float64)))
    idx = np.arange(chunk, dtype=np.float64)
    diff = idx[:, None] - idx[None, :]
    intra = np.where(diff[None] >= 0, np.exp(log_gamma[:, None, None] * np.maximum(diff, 0.0)[None]), 0.0)
    xi = np.exp(log_gamma[:, None] * (idx + 1.0))[:, :, None]
    zeta = np.exp(log_gamma[:, None] * (chunk - 1.0 - idx))[:, :, None]
    gamma_c = tuple(float(g) for g in np.exp(log_gamma * chunk))
    return (jnp.asarray(intra, F32), jnp.asarray(xi, F32), jnp.asarray(zeta, F32), gamma_c)


def _retention_chunk(q, k, v_halves, rows, intra_ref, xi_ref, zeta_ref, o_ref, state_ref, gamma_c):
    dk = q.shape[-1] // RET_HEADS
    dv = 2 * v_halves[0].shape[-1] // RET_HEADS
    per_half = RET_HEADS // 2
    first = None
    for h in range(RET_HEADS):
        qh = q[rows, h * dk:(h + 1) * dk]
        kh = k[rows, h * dk:(h + 1) * dk]
        vh = v_halves[h // per_half][rows, (h % per_half) * dv:(h % per_half + 1) * dv]
        s = lax.dot_general(qh, kh, (((1,), (1,)), ((), ())), preferred_element_type=F32) * intra_ref[h]
        st = state_ref[h]
        o = _dot(s.astype(BF16), vh) + _dot(qh, st.astype(BF16)) * xi_ref[h]
        o_ref[rows, h * dv:(h + 1) * dv] = o.astype(o_ref.dtype)
        kz = (kh.astype(F32) * zeta_ref[h]).astype(BF16)
        state_ref[h] = st * gamma_c[h] + lax.dot_general(
            kz, vh, (((0,), (0,)), ((), ())), preferred_element_type=F32)
        first = o if first is None else first
    return first


def _inproj_kernel(contig_ref, x_ref, mod_ref, pos_ref, freq_ref, ctab_ref, stab_ref, g_ref, w_ref, wdw_ref,
                   bdw_ref, intra_ref, xi_ref, zeta_ref, o_ref, gret_ref, gr_ref, gc_ref, cv_ref,
                   win_ref, cos_ref, sin_ref, state_ref, *, k_scale, gamma_c):
    tm, D = x_ref.shape
    dk = D // RET_HEADS
    half = dk // 2
    chunk = intra_ref.shape[-1]

    @pl.when(pl.program_id(1) == 0)
    def _():
        win_ref[:, 0:CONV_HALO, :] = jnp.zeros((D // LANES, CONV_HALO, LANES), F32)
        state_ref[...] = jnp.zeros(state_ref.shape, F32)

    _rope_tables(contig_ref, pos_ref, freq_ref, ctab_ref, stab_ref, cos_ref, sin_ref)

    x = x_ref[...]
    shift, scale, _ = _mod_slices(mod_ref, 1, D)
    xb = (_rms_norm(x, g_ref[...] * (1.0 + scale)) + shift).astype(BF16)

    def proj(c0):
        return _dot(xb, w_ref[:, c0:c0 + D])

    def anchor_of(z):
        return z[0:1, 0:LANES] * 0.0

    glu_a = proj(6 * D)
    glu_b = proj(7 * D)
    u = glu_a * _sigmoid(glu_b)
    for cb in range(D // LANES):
        win_ref[cb, CONV_HALO:CONV_HALO + tm, :] = u[:, cb * LANES:(cb + 1) * LANES]

    def rope(z, mult):
        cos = cos_ref[...]
        sin = sin_ref[...]
        if mult != 1.0:
            cos, sin = cos * mult, sin * mult
        parts = []
        for h in range(RET_HEADS):
            x1 = z[:, h * dk:h * dk + half]
            x2 = z[:, h * dk + half:(h + 1) * dk]
            parts.append((x1 * cos - x2 * sin).astype(BF16))
            parts.append((x1 * sin + x2 * cos).astype(BF16))
        return jnp.concatenate(parts, axis=1)

    n_conv = 6
    tiles = tm // (2 * SUBLANES)
    sizes = [2 * SUBLANES * (tiles // n_conv + (1 if j < tiles % n_conv else 0)) for j in range(n_conv)]
    starts = [sum(sizes[:j]) for j in range(n_conv)]

    def conv_block(j, z):
        _causal_conv_rows(win_ref, wdw_ref, bdw_ref, cv_ref, starts[j], sizes[j], anchor_of(z))

    q = rope(proj(0), 1.0)
    k = rope(proj(D), k_scale)
    zv0 = proj(2 * D)
    zv1 = proj(3 * D)
    conv_block(0, zv1)
    v_halves = (zv0.astype(BF16), zv1.astype(BF16))

    for c in range(tm // chunk):
        rows = slice(c * chunk, (c + 1) * chunk)
        first = _retention_chunk(q, k, v_halves, rows, intra_ref, xi_ref, zeta_ref, o_ref, state_ref, gamma_c)
        conv_block(1 + 2 * c, first)
        z = proj((4 + c) * D)
        conv_block(2 + 2 * c, z)
        gret_ref[:, c * D:(c + 1) * D] = z.astype(BF16)
    z = proj(8 * D)
    conv_block(5, z)
    gr_ref[...] = z.astype(BF16)
    gc_ref[...] = proj(9 * D).astype(BF16)

    win_ref[:, 0:CONV_HALO, :] = win_ref[:, tm:tm + CONV_HALO, :]


def _inproj(h, mod3, positions, g_norm, w_in, w_dw, b_dw, tm=512, chunk=RET_CHUNK):
    B, S, D = h.shape
    dk = D // RET_HEADS
    dv = 2 * dk
    half = dk // 2
    nt = S // tm
    assert tm // chunk == 2, "the stage plan interleaves exactly two retention chunks per tile"
    freqs = np.power(ROPE_BASE, -np.arange(0, half, dtype=np.float64) * (2.0 / dk))
    step = np.arange(tm, dtype=np.float64)[:, None] * freqs[None, :]
    ctab, stab = jnp.asarray(np.cos(step), F32), jnp.asarray(np.sin(step), F32)
    contig = jnp.all(positions[:, 1:] - positions[:, :-1] == 1).astype(jnp.int32).reshape(1, 1)
    intra, xi, zeta, gamma_c = _retention_tables(chunk)
    row = lambda n: pl.BlockSpec((None, tm, n), lambda b, i: (b, i, 0))
    out = lambda n: jax.ShapeDtypeStruct((B, S, n), BF16)
    kern = functools.partial(_inproj_kernel, k_scale=float(dk) ** -0.5, gamma_c=gamma_c)
    return pl.pallas_call(
        kern,
        grid=(B, nt),
        in_specs=[pl.BlockSpec(memory_space=pltpu.SMEM),
                  row(D),
                  pl.BlockSpec((None, 1, mod3.shape[-1]), lambda b, i: (b, 0, 0)),
                  row(1),
                  _resident((1, half)), _resident((tm, half)), _resident((tm, half)),
                  _resident((1, D)),
                  _resident(w_in.shape),
                  _resident((CONV_WIDTH, D)), _resident((1, D)),
                  _resident(intra.shape), _resident(xi.shape), _resident(zeta.shape)],
        out_specs=[row(2 * D), row(2 * D), row(D), row(D), row(D)],
        out_shape=[out(2 * D), out(2 * D), out(D), out(D), out(D)],
        scratch_shapes=[pltpu.VMEM((D // LANES, CONV_HALO + tm, LANES), F32), pltpu.VMEM((tm, half), F32),
                        pltpu.VMEM((tm, half), F32), pltpu.VMEM((RET_HEADS, dk, dv), F32)],
        compiler_params=_params("parallel", "arbitrary", vmem_limit_bytes=INPROJ_VMEM_LIMIT_BYTES),
        name="inproj",
    )(contig, h, mod3, positions.reshape(B, S, 1), jnp.asarray(freqs, F32).reshape(1, half), ctab, stab,
      g_norm.reshape(1, D), w_in, w_dw, b_dw.reshape(1, D), intra, xi, zeta)


def _merge_pieces(h1_ref, mod_ref, o_ref_in, gret_ref, gr_ref, gc_ref, cv_ref, gng_ref, gnb_ref,
                  lng_ref, lnb_ref, wro_ref, wco_ref, bco_ref, wout_ref, h2_out):
    D = h1_ref.shape[-1]
    dv = o_ref_in.shape[-1] // RET_HEADS
    state = {}

    def conv_tail(zero):
        cv = cv_ref[...].astype(F32)
        mu = jnp.mean(cv, axis=-1, keepdims=True)
        d = cv - mu
        var = jnp.mean(d * d, axis=-1, keepdims=True)
        ln = d * lax.rsqrt(var + EPS) * lng_ref[...] + (lnb_ref[...] + zero)
        state["y_conv"] = _dot(_silu(ln).astype(BF16), wco_ref[...]) + bco_ref[...]

    def head_tail(h):
        def run(zero):
            cols = slice(h * dv, (h + 1) * dv)
            o = o_ref_in[:, cols].astype(F32)
            mu = jnp.mean(o, axis=-1, keepdims=True)
            d = o - mu
            var = jnp.mean(d * d, axis=-1, keepdims=True)
            y = d * lax.rsqrt(var + EPS) * gng_ref[:, cols] + (gnb_ref[:, cols] + zero)
            og = (_silu(gret_ref[:, cols].astype(F32)) * y).astype(BF16)
            part = _dot(og, wro_ref[cols, :])
            state["y_ret"] = part if h == 0 else state["y_ret"] + part
        return run

    def merge(zero):
        _, _, gate = _mod_slices(mod_ref, 0, D)
        merged = (_sigmoid(gr_ref[...].astype(F32) + zero) * state["y_ret"]
                  + _sigmoid(gc_ref[...].astype(F32) + zero) * state["y_conv"])
        h2_out(h1_ref[...] + gate * _dot(merged.astype(BF16), wout_ref[...]))

    return [conv_tail] + [head_tail(h) for h in range(RET_HEADS)] + [merge]


def _tail_kernel(h1_ref, modm_ref, o_ref_in, gret_ref, gr_ref, gc_ref, cv_ref, modf_ref,
                 gng_ref, gnb_ref, lng_ref, lnb_ref, wro_ref, wco_ref, bco_ref, wout_ref,
                 g3_ref, wgu_ref, wd_ref, gf_ref, out_ref, h2_ref, *, ck, final_norm):
    t = pl.program_id(0)

    @pl.when(t == 0)
    def _():
        h2_ref[...] = jnp.zeros(h2_ref.shape, F32)

    def h2_out(h2):
        h2_ref[t % 2] = h2

    side = _merge_pieces(h1_ref, modm_ref, o_ref_in, gret_ref, gr_ref, gc_ref, cv_ref, gng_ref, gnb_ref,
                         lng_ref, lnb_ref, wro_ref, wco_ref, bco_ref, wout_ref, h2_out)
    out_ref[...] = _ffn_math(h2_ref[(t + 1) % 2], modf_ref, g3_ref, wgu_ref, wd_ref, gf_ref,
                             sub=0, ck=ck, final_norm=final_norm, side=side)


def _tail(h1, mod3, o, gret, gr, gc, cv, gn_g, gn_b, ln_g, ln_b, w_ret_o, w_conv_o, b_conv_o, w_out,
          g_norm3, w_gu, w_d, g_final, *, final_norm, tm=512, ck=256):
    B, S, D = h1.shape
    V = o.shape[-1]
    Fh = w_d.shape[0]
    ntb = S // tm
    last_tile = B * ntb - 1

    def merged_tile(t):
        m = jnp.minimum(t, last_tile)
        return m // ntb, m % ntb

    def ffn_tile(t):
        f = jnp.maximum(t - 1, 0)
        return f // ntb, f % ntb

    row = lambda n: pl.BlockSpec((None, tm, n), lambda t: (*merged_tile(t), 0))
    mod_merge, mod_ffn = mod3[:, :, 3 * D:6 * D], mod3[:, :, 6 * D:9 * D]
    mod_block = lambda tile: pl.BlockSpec((None, 1, 3 * D), lambda t: (tile(t)[0], 0, 0))
    kern = functools.partial(_tail_kernel, ck=ck, final_norm=final_norm)
    return pl.pallas_call(
        kern,
        grid=(last_tile + 2,),
        in_specs=[row(D), mod_block(merged_tile), row(V), row(V), row(D), row(D), row(D), mod_block(ffn_tile),
                  _resident((1, V)), _resident((1, V)), _resident((1, D)), _resident((1, D)),
                  _resident((V, D)), _resident((D, D)), _resident((1, D)), _resident((D, D)),
                  _resident((1, D)), _resident((D, 2 * Fh)), _resident((Fh, D)), _resident((1, D))],
        out_specs=pl.BlockSpec((None, tm, D), lambda t: (*ffn_tile(t), 0)),
        out_shape=jax.ShapeDtypeStruct((B, S, D), F32),
        scratch_shapes=[pltpu.VMEM((2, tm, D), F32)],
        compiler_params=_params("arbitrary", vmem_limit_bytes=TAIL_VMEM_LIMIT_BYTES),
        name="tail",
    )(h1, mod_merge, o, gret, gr, gc, cv, mod_ffn, gn_g.reshape(1, V), gn_b.reshape(1, V), ln_g.reshape(1, D),
      ln_b.reshape(1, D), w_ret_o, w_conv_o, b_conv_o.reshape(1, D), w_out,
      g_norm3.reshape(1, D), w_gu, w_d, g_final.reshape(1, D))


def kernel(x, c, positions, w_mod, b_mod, g_norm1, w_ffn1_gu, w_ffn1_d, g_norm2, w_in, ret_gn_g, ret_gn_b,
           w_ret_o, w_dw, b_dw, conv_ln_g, conv_ln_b, w_conv_o, b_conv_o, w_out, g_norm3, w_ffn2_gu,
           w_ffn2_d, g_normf):
    B, S, D = x.shape
    depth = w_mod.shape[0]

    h = x
    for l in range(depth):
        last = l == depth - 1
        mod3 = _mod(c, w_mod[l], b_mod[l]).reshape(B, 1, 3 * N_SUB * D)
        h, (w_in_b, w_ro_b, w_co_b, w_out_b, w_gu2_b, w_d2_b) = _ffn(
            h, mod3, g_norm1[l], w_ffn1_gu[l], w_ffn1_d[l], g_normf,
            (w_in[l], w_ret_o[l], w_conv_o[l], w_out[l], w_ffn2_gu[l], w_ffn2_d[l]),
            sub=0, final_norm=False)
        o, gret, gr, gc, cv = _inproj(h, mod3, positions, g_norm2[l], w_in_b,
                                      w_dw[l].reshape(CONV_WIDTH, D), b_dw[l])
        h = _tail(h, mod3, o, gret, gr, gc, cv, ret_gn_g[l], ret_gn_b[l], conv_ln_g[l], conv_ln_b[l],
                  w_ro_b, w_co_b, b_conv_o[l], w_out_b,
                  g_norm3[l], w_gu2_b, w_d2_b, g_normf, final_norm=last)
    return h
```

```python
import functools

import numpy as np
import jax
import jax.numpy as jnp
from jax import lax
from jax.experimental import pallas as pl
from jax.experimental.pallas import tpu as pltpu

F32 = jnp.float32
BF16 = jnp.bfloat16

EPS = 1e-6
N_SUB = 3
RET_HEADS = 4
CONV_WIDTH = 31
ROPE_BASE = 10000.0
RET_CHUNK = 256
CONV_HALO = 32
LANES = 128
SUBLANES = 8
VMEM_LIMIT_BYTES = 56 * 1024 * 1024
TAIL_VMEM_LIMIT_BYTES = 62 * 1024 * 1024
INPROJ_VMEM_LIMIT_BYTES = 60 * 1024 * 1024


def _sigmoid(x):
    return 0.5 * jnp.tanh(0.5 * x) + 0.5


def _silu(x):
    h = 0.5 * x
    return h + h * jnp.tanh(h)


def _rms_norm(x, g):
    return x * lax.rsqrt(jnp.mean(x * x, axis=-1, keepdims=True) + EPS) * g


def _dot(a, b):
    return jnp.dot(a, b, preferred_element_type=F32)


def _resident(shape):
    zeros = (0,) * len(shape)
    return pl.BlockSpec(shape, lambda *_: zeros, pipeline_mode=pl.Buffered(1))


def _params(*semantics, vmem_limit_bytes=VMEM_LIMIT_BYTES):
    return pltpu.CompilerParams(dimension_semantics=semantics, vmem_limit_bytes=vmem_limit_bytes)


def _mod_kernel(c_ref, w_ref, b_ref, o_ref):
    c = c_ref[...]
    o_ref[...] = _dot(_silu(c).astype(BF16), w_ref[...].astype(BF16)) + b_ref[...]


def _mod(c, w_mod, b_mod, tn=1024):
    B, D = c.shape
    N = w_mod.shape[1]
    return pl.pallas_call(
        _mod_kernel,
        grid=(N // tn,),
        in_specs=[_resident((B, D)),
                  pl.BlockSpec((D, tn), lambda j: (0, j)),
                  pl.BlockSpec((1, tn), lambda j: (0, j))],
        out_specs=pl.BlockSpec((B, tn), lambda j: (0, j)),
        out_shape=jax.ShapeDtypeStruct((B, N), F32),
        compiler_params=_params("parallel"),
        name="mod",
    )(c, w_mod, b_mod.reshape(1, N))


def _mod_slices(mod_ref, sub, D):
    base = 3 * sub * D
    return (mod_ref[:, base:base + D], mod_ref[:, base + D:base + 2 * D],
            mod_ref[:, base + 2 * D:base + 3 * D])


def _bf16(w):
    return w if w.dtype == BF16 else w.astype(BF16)


def _zero_of(v):
    return v[0:1, 0:1] * 0.0


def _ffn_math(x, mod_ref, g_ref, wgu_ref, wd_ref, gf_ref, *, sub, ck, final_norm, side=()):
    D = x.shape[-1]
    Fh = wd_ref.shape[0]
    shift, scale, gate = _mod_slices(mod_ref, sub, D)
    xb = (_rms_norm(x, g_ref[...] * (1.0 + scale)) + shift).astype(BF16)
    acc = None
    zeros = []
    for c0 in range(0, Fh, ck):
        c1 = min(c0 + ck, Fh)
        g = _dot(xb, _bf16(wgu_ref[:, c0:c1]))
        u = _dot(xb, _bf16(wgu_ref[:, Fh + c0:Fh + c1]))
        zeros.append(_zero_of(g))
        part = _dot((_silu(g) * u).astype(BF16), _bf16(wd_ref[c0:c1, :]))
        acc = part if acc is None else acc + part
    h = x + (0.5 * gate) * acc
    if final_norm:
        h = _rms_norm(h, gf_ref[...])
    for piece, zero in zip(side, zeros):
        piece(zero)
    return h


def _ffn_kernel(x_ref, mod_ref, g_ref, wgu_ref, wd_ref, gf_ref, *rest, sub, ck, final_norm):
    n_cast = (len(rest) - 1) // 2
    o_ref = rest[n_cast]
    for src_ref, dst_ref in zip(rest[:n_cast], rest[n_cast + 1:]):
        dst_ref[...] = src_ref[...].astype(BF16)
    o_ref[...] = _ffn_math(x_ref[...], mod_ref, g_ref, wgu_ref, wd_ref, gf_ref,
                           sub=sub, ck=ck, final_norm=final_norm)


def _ffn(x, mod3, g_norm, w_gu, w_d, g_final, later_weights=(), *, sub, final_norm, tm=512, ck=256):
    B, S, D = x.shape
    Fh = w_d.shape[0]
    nt = S // tm
    steps = B * nt
    row = pl.BlockSpec((None, tm, D), lambda b, i: (b, i, 0))
    rides = [w.shape[0] % (steps * 2 * SUBLANES) == 0 for w in later_weights]
    riders = [w for w, r in zip(later_weights, rides) if r]
    cast_specs = [pl.BlockSpec((w.shape[0] // steps, w.shape[1]), lambda b, i: (b * nt + i, 0))
                  for w in riders]
    kern = functools.partial(_ffn_kernel, sub=sub, ck=ck, final_norm=final_norm)
    h, *cast = pl.pallas_call(
        kern,
        grid=(B, nt),
        in_specs=[row,
                  pl.BlockSpec((None, 1, mod3.shape[-1]), lambda b, i: (b, 0, 0)),
                  _resident((1, D)),
                  _resident((D, 2 * Fh)),
                  _resident((Fh, D)),
                  _resident((1, D))] + cast_specs,
        out_specs=[row] + cast_specs,
        out_shape=[jax.ShapeDtypeStruct((B, S, D), F32)]
                  + [jax.ShapeDtypeStruct(w.shape, BF16) for w in riders],
        compiler_params=_params("parallel", "parallel"),
        name=f"ffn{sub}",
    )(x, mod3, g_norm.reshape(1, D), w_gu, w_d, g_final.reshape(1, D), *riders)
    cast = iter(cast)
    return h, [next(cast) if r else w.astype(BF16) for w, r in zip(later_weights, rides)]


def _causal_conv_rows(win_ref, wdw_ref, bdw_ref, cv_ref, r0, rb, anchor):
    D = cv_ref.shape[-1]
    off0 = CONV_HALO - (CONV_WIDTH - 1)
    for c0 in range(0, D, LANES):
        cb = c0 // LANES
        acc = jnp.broadcast_to(bdw_ref[:, c0:c0 + LANES] + anchor, (rb, LANES))
        for m in range(SUBLANES):
            rows = rb if m == 0 else rb + SUBLANES
            g = None
            for p in range(m, off0 + CONV_WIDTH, SUBLANES):
                if p < off0:
                    continue
                a0 = r0 + p - m
                term = win_ref[cb, a0:a0 + rows, :] * wdw_ref[p - off0:p - off0 + 1, c0:c0 + LANES]
                g = term if g is None else g + term
            acc = acc + g[m:m + rb]
        cv_ref[r0:r0 + rb, c0:c0 + LANES] = acc.astype(cv_ref.dtype)


def _rope_tables(contig_ref, pos_ref, freq_ref, ctab_ref, stab_ref, cos_ref, sin_ref):
    @pl.when(contig_ref[0, 0] != 0)
    def _():
        a0 = pos_ref[0:1, :].astype(F32) * freq_ref[...]
        c0, s0 = jnp.cos(a0), jnp.sin(a0)
        ct, st = ctab_ref[...], stab_ref[...]
        cos_ref[...] = c0 * ct - s0 * st
        sin_ref[...] = s0 * ct + c0 * st

    @pl.when(contig_ref[0, 0] == 0)
    def _():
        ang = pos_ref[...].astype(F32) * freq_ref[...]
        cos_ref[...] = jnp.cos(ang)
        sin_ref[...] = jnp.sin(ang)


def _retention_tables(chunk):
    log_gamma = np.log1p(-np.power(2.0, -5.0 - np.arange(RET_HEADS, dtype=np.float64)))
    idx = np.arange(chunk, dtype=np.float64)
    diff = idx[:, None] - idx[None, :]
    intra = np.where(diff[None] >= 0, np.exp(log_gamma[:, None, None] * np.maximum(diff, 0.0)[None]), 0.0)
    xi = np.exp(log_gamma[:, None] * (idx + 1.0))[:, :, None]
    zeta = np.exp(log_gamma[:, None] * (chunk - 1.0 - idx))[:, :, None]
    gamma_c = tuple(float(g) for g in np.exp(log_gamma * chunk))
    return (jnp.asarray(intra, F32), jnp.asarray(xi, F32), jnp.asarray(zeta, F32), gamma_c)


def _retention_chunk(q, k, v_halves, rows, intra_ref, xi_ref, zeta_ref, o_ref, state_ref, gamma_c):
    dk = q.shape[-1] // RET_HEADS
    dv = 2 * v_halves[0].shape[-1] // RET_HEADS
    per_half = RET_HEADS // 2
    first = None
    for h in range(RET_HEADS):
        qh = q[rows, h * dk:(h + 1) * dk]
        kh = k[rows, h * dk:(h + 1) * dk]
        vh = v_halves[h // per_half][rows, (h % per_half) * dv:(h % per_half + 1) * dv]
        s = lax.dot_general(qh, kh, (((1,), (1,)), ((), ())), preferred_element_type=F32) * intra_ref[h]
        st = state_ref[h]
        o = _dot(s.astype(BF16), vh) + _dot(qh, st.astype(BF16)) * xi_ref[h]
        o_ref[rows, h * dv:(h + 1) * dv] = o.astype(o_ref.dtype)
        kz = (kh.astype(F32) * zeta_ref[h]).astype(BF16)
        state_ref[h] = st * gamma_c[h] + lax.dot_general(
            kz, vh, (((0,), (0,)), ((), ())), preferred_element_type=F32)
        first = o if first is None else first
    return first


def _inproj_kernel(contig_ref, x_ref, mod_ref, pos_ref, freq_ref, ctab_ref, stab_ref, g_ref, w_ref, wdw_ref,
                   bdw_ref, intra_ref, xi_ref, zeta_ref, o_ref, gret_ref, gr_ref, gc_ref, cv_ref,
                   win_ref, cos_ref, sin_ref, state_ref, *, k_scale, gamma_c):
    tm, D = x_ref.shape
    dk = D // RET_HEADS
    half = dk // 2
    chunk = intra_ref.shape[-1]

    @pl.when(pl.program_id(1) == 0)
    def _():
        win_ref[:, 0:CONV_HALO, :] = jnp.zeros((D // LANES, CONV_HALO, LANES), F32)
        state_ref[...] = jnp.zeros(state_ref.shape, F32)

    _rope_tables(contig_ref, pos_ref, freq_ref, ctab_ref, stab_ref, cos_ref, sin_ref)

    x = x_ref[...]
    shift, scale, _ = _mod_slices(mod_ref, 1, D)
    xb = (_rms_norm(x, g_ref[...] * (1.0 + scale)) + shift).astype(BF16)

    def proj(c0):
        return _dot(xb, w_ref[:, c0:c0 + D])

    def anchor_of(z):
        return z[0:1, 0:LANES] * 0.0

    glu_a = proj(6 * D)
    glu_b = proj(7 * D)
    u = glu_a * _sigmoid(glu_b)
    for cb in range(D // LANES):
        win_ref[cb, CONV_HALO:CONV_HALO + tm, :] = u[:, cb * LANES:(cb + 1) * LANES]

    def rope(z, mult):
        cos = cos_ref[...]
        sin = sin_ref[...]
        parts = []
        for h in range(RET_HEADS):
            x1 = z[:, h * dk:h * dk + half]
            x2 = z[:, h * dk + half:(h + 1) * dk]
            parts.append(((x1 * cos - x2 * sin) * mult).astype(BF16))
            parts.append(((x1 * sin + x2 * cos) * mult).astype(BF16))
        return jnp.concatenate(parts, axis=1)

    n_conv = 6
    tiles = tm // (2 * SUBLANES)
    sizes = [2 * SUBLANES * (tiles // n_conv + (1 if j < tiles % n_conv else 0)) for j in range(n_conv)]
    starts = [sum(sizes[:j]) for j in range(n_conv)]

    def conv_block(j, z):
        _causal_conv_rows(win_ref, wdw_ref, bdw_ref, cv_ref, starts[j], sizes[j], anchor_of(z))

    q = rope(proj(0), 1.0)
    k = rope(proj(D), k_scale)
    zv0 = proj(2 * D)
    zv1 = proj(3 * D)
    conv_block(0, zv1)
    v_halves = (zv0.astype(BF16), zv1.astype(BF16))

    for c in range(tm // chunk):
        rows = slice(c * chunk, (c + 1) * chunk)
        first = _retention_chunk(q, k, v_halves, rows, intra_ref, xi_ref, zeta_ref, o_ref, state_ref, gamma_c)
        conv_block(1 + 2 * c, first)
        z = proj((4 + c) * D)
        conv_block(2 + 2 * c, z)
        gret_ref[:, c * D:(c + 1) * D] = z.astype(BF16)
    z = proj(8 * D)
    conv_block(5, z)
    gr_ref[...] = z.astype(BF16)
    gc_ref[...] = proj(9 * D).astype(BF16)

    win_ref[:, 0:CONV_HALO, :] = win_ref[:, tm:tm + CONV_HALO, :]


def _inproj(h, mod3, positions, g_norm, w_in, w_dw, b_dw, tm=512, chunk=RET_CHUNK):
    B, S, D = h.shape
    dk = D // RET_HEADS
    dv = 2 * dk
    half = dk // 2
    nt = S // tm
    assert tm // chunk == 2, "the stage plan interleaves exactly two retention chunks per tile"
    freqs = np.power(ROPE_BASE, -np.arange(0, half, dtype=np.float64) * (2.0 / dk))
    step = np.arange(tm, dtype=np.float64)[:, None] * freqs[None, :]
    ctab, stab = jnp.asarray(np.cos(step), F32), jnp.asarray(np.sin(step), F32)
    contig = jnp.all(positions[:, 1:] - positions[:, :-1] == 1).astype(jnp.int32).reshape(1, 1)
    intra, xi, zeta, gamma_c = _retention_tables(chunk)
    row = lambda n: pl.BlockSpec((None, tm, n), lambda b, i: (b, i, 0))
    out = lambda n: jax.ShapeDtypeStruct((B, S, n), BF16)
    kern = functools.partial(_inproj_kernel, k_scale=float(dk) ** -0.5, gamma_c=gamma_c)
    return pl.pallas_call(
        kern,
        grid=(B, nt),
        in_specs=[pl.BlockSpec(memory_space=pltpu.SMEM),
                  row(D),
                  pl.BlockSpec((None, 1, mod3.shape[-1]), lambda b, i: (b, 0, 0)),
                  row(1),
                  _resident((1, half)), _resident((tm, half)), _resident((tm, half)),
                  _resident((1, D)),
                  _resident(w_in.shape),
                  _resident((CONV_WIDTH, D)), _resident((1, D)),
                  _resident(intra.shape), _resident(xi.shape), _resident(zeta.shape)],
        out_specs=[row(2 * D), row(2 * D), row(D), row(D), row(D)],
        out_shape=[out(2 * D), out(2 * D), out(D), out(D), out(D)],
        scratch_shapes=[pltpu.VMEM((D // LANES, CONV_HALO + tm, LANES), F32), pltpu.VMEM((tm, half), F32),
                        pltpu.VMEM((tm, half), F32), pltpu.VMEM((RET_HEADS, dk, dv), F32)],
        compiler_params=_params("parallel", "arbitrary", vmem_limit_bytes=INPROJ_VMEM_LIMIT_BYTES),
        name="inproj",
    )(contig, h, mod3, positions.reshape(B, S, 1), jnp.asarray(freqs, F32).reshape(1, half), ctab, stab,
      g_norm.reshape(1, D), w_in, w_dw, b_dw.reshape(1, D), intra, xi, zeta)


def _merge_pieces(h1_ref, mod_ref, o_ref_in, gret_ref, gr_ref, gc_ref, cv_ref, gng_ref, gnb_ref,
                  lng_ref, lnb_ref, wro_ref, wco_ref, bco_ref, wout_ref, h2_out):
    D = h1_ref.shape[-1]
    dv = o_ref_in.shape[-1] // RET_HEADS
    state = {}

    def conv_tail(zero):
        cv = cv_ref[...].astype(F32)
        mu = jnp.mean(cv, axis=-1, keepdims=True)
        d = cv - mu
        var = jnp.mean(d * d, axis=-1, keepdims=True)
        ln = d * lax.rsqrt(var + EPS) * lng_ref[...] + (lnb_ref[...] + zero)
        state["y_conv"] = _dot(_silu(ln).astype(BF16), wco_ref[...]) + bco_ref[...]

    def head_tail(h):
        def run(zero):
            cols = slice(h * dv, (h + 1) * dv)
            o = o_ref_in[:, cols].astype(F32)
            mu = jnp.mean(o, axis=-1, keepdims=True)
            d = o - mu
            var = jnp.mean(d * d, axis=-1, keepdims=True)
            y = d * lax.rsqrt(var + EPS) * gng_ref[:, cols] + (gnb_ref[:, cols] + zero)
            og = (_silu(gret_ref[:, cols].astype(F32)) * y).astype(BF16)
            part = _dot(og, wro_ref[cols, :])
            state["y_ret"] = part if h == 0 else state["y_ret"] + part
        return run

    def merge(zero):
        _, _, gate = _mod_slices(mod_ref, 0, D)
        merged = (_sigmoid(gr_ref[...].astype(F32) + zero) * state["y_ret"]
                  + _sigmoid(gc_ref[...].astype(F32) + zero) * state["y_conv"])
        h2_out(h1_ref[...] + gate * _dot(merged.astype(BF16), wout_ref[...]))

    return [conv_tail] + [head_tail(h) for h in range(RET_HEADS)] + [merge]


def _tail_kernel(h1_ref, modm_ref, o_ref_in, gret_ref, gr_ref, gc_ref, cv_ref, modf_ref,
                 gng_ref, gnb_ref, lng_ref, lnb_ref, wro_ref, wco_ref, bco_ref, wout_ref,
                 g3_ref, wgu_ref, wd_ref, gf_ref, out_ref, h2_ref, *, ck, final_norm):
    t = pl.program_id(0)

    @pl.when(t == 0)
    def _():
        h2_ref[...] = jnp.zeros(h2_ref.shape, F32)

    def h2_out(h2):
        h2_ref[t % 2] = h2

    side = _merge_pieces(h1_ref, modm_ref, o_ref_in, gret_ref, gr_ref, gc_ref, cv_ref, gng_ref, gnb_ref,
                         lng_ref, lnb_ref, wro_ref, wco_ref, bco_ref, wout_ref, h2_out)
    out_ref[...] = _ffn_math(h2_ref[(t + 1) % 2], modf_ref, g3_ref, wgu_ref, wd_ref, gf_ref,
                             sub=0, ck=ck, final_norm=final_norm, side=side)


def _tail(h1, mod3, o, gret, gr, gc, cv, gn_g, gn_b, ln_g, ln_b, w_ret_o, w_conv_o, b_conv_o, w_out,
          g_norm3, w_gu, w_d, g_final, *, final_norm, tm=512, ck=256):
    B, S, D = h1.shape
    V = o.shape[-1]
    Fh = w_d.shape[0]
    ntb = S // tm
    last_tile = B * ntb - 1

    def merged_tile(t):
        m = jnp.minimum(t, last_tile)
        return m // ntb, m % ntb

    def ffn_tile(t):
        f = jnp.maximum(t - 1, 0)
        return f // ntb, f % ntb

    row = lambda n: pl.BlockSpec((None, tm, n), lambda t: (*merged_tile(t), 0))
    mod_merge, mod_ffn = mod3[:, :, 3 * D:6 * D], mod3[:, :, 6 * D:9 * D]
    mod_block = lambda tile: pl.BlockSpec((None, 1, 3 * D), lambda t: (tile(t)[0], 0, 0))
    kern = functools.partial(_tail_kernel, ck=ck, final_norm=final_norm)
    return pl.pallas_call(
        kern,
        grid=(last_tile + 2,),
        in_specs=[row(D), mod_block(merged_tile), row(V), row(V), row(D), row(D), row(D), mod_block(ffn_tile),
                  _resident((1, V)), _resident((1, V)), _resident((1, D)), _resident((1, D)),
                  _resident((V, D)), _resident((D, D)), _resident((1, D)), _resident((D, D)),
                  _resident((1, D)), _resident((D, 2 * Fh)), _resident((Fh, D)), _resident((1, D))],
        out_specs=pl.BlockSpec((None, tm, D), lambda t: (*ffn_tile(t), 0)),
        out_shape=jax.ShapeDtypeStruct((B, S, D), F32),
        scratch_shapes=[pltpu.VMEM((2, tm, D), F32)],
        compiler_params=_params("arbitrary", vmem_limit_bytes=TAIL_VMEM_LIMIT_BYTES),
        name="tail",
    )(h1, mod_merge, o, gret, gr, gc, cv, mod_ffn, gn_g.reshape(1, V), gn_b.reshape(1, V), ln_g.reshape(1, D),
      ln_b.reshape(1, D), w_ret_o, w_conv_o, b_conv_o.reshape(1, D), w_out,
      g_norm3.reshape(1, D), w_gu, w_d, g_final.reshape(1, D))


def kernel(x, c, positions, w_mod, b_mod, g_norm1, w_ffn1_gu, w_ffn1_d, g_norm2, w_in, ret_gn_g, ret_gn_b,
           w_ret_o, w_dw, b_dw, conv_ln_g, conv_ln_b, w_conv_o, b_conv_o, w_out, g_norm3, w_ffn2_gu,
           w_ffn2_d, g_normf):
    B, S, D = x.shape
    depth = w_mod.shape[0]

    h = x
    for l in range(depth):
        last = l == depth - 1
        mod3 = _mod(c, w_mod[l], b_mod[l]).reshape(B, 1, 3 * N_SUB * D)
        h, (w_in_b, w_ro_b, w_co_b, w_out_b, w_gu2_b, w_d2_b) = _ffn(
            h, mod3, g_norm1[l], w_ffn1_gu[l], w_ffn1_d[l], g_normf,
            (w_in[l], w_ret_o[l], w_conv_o[l], w_out[l], w_ffn2_gu[l], w_ffn2_d[l]),
            sub=0, final_norm=False)
        o, gret, gr, gc, cv = _inproj(h, mod3, positions, g_norm2[l], w_in_b,
                                      w_dw[l].reshape(CONV_WIDTH, D), b_dw[l])
        h = _tail(h, mod3, o, gret, gr, gc, cv, ret_gn_g[l], ret_gn_b[l], conv_ln_g[l], conv_ln_b[l],
                  w_ro_b, w_co_b, b_conv_o[l], w_out_b,
                  g_norm3[l], w_gu2_b, w_d2_b, g_normf, final_norm=last)
    return h
```
